```python
import math
import jax, jax.numpy as jnp
from jax import lax
import numpy as np

D_MODEL = 1024
BATCH = 8
SEQ = 4096
DEPTH = 4

D_CONV = D_MODEL // 2
CONV_WIDTH = 3
D_SSM = D_MODEL // 2
SSM_GROUP = 16
N_SSM_GROUPS = D_SSM // SSM_GROUP
SSM_STATE = 64
D_POOL = D_MODEL // 2
POOL_WINDOWS = (2, 4, 8, 16)
POOL_GROUP = D_POOL // len(POOL_WINDOWS)
D_SGU = D_MODEL // 2
SGU_HEADS = 4
SGU_HEAD_DIM = D_SGU // SGU_HEADS
CHUNK = 128
D_FF = ((8 * D_MODEL // 3 + 127) // 128) * 128
N_EVEN = (DEPTH + 1) // 2
N_ODD = DEPTH // 2
EPS = 1e-6

kernel_name = "hybrid_conv_s5_pool_sgu_trunk"


def rmsnorm(x, g):
    xf = x.astype(jnp.float32)
    y = xf * lax.rsqrt(jnp.mean(xf * xf, axis=-1, keepdims=True) + EPS)
    return (y * g.astype(jnp.float32)).astype(x.dtype)


def causal_dwconv(x, w):
    L = x.shape[1]
    K = w.shape[0]
    xp = jnp.pad(x, ((0, 0), (K - 1, 0), (0, 0)))
    y = xp[:, 0:L] * w[0]
    for k in range(1, K):
        y = y + xp[:, k:k + L] * w[k]
    return y


def short_conv_mixer(xa, ba, ca, conv_w):
    return ba * causal_dwconv(ca * xa, conv_w)


def s5_mixer(u, log_step, a_re, a_im, b_re, b_im, c_re, c_im, d_skip, glu_w, glu_b):
    f32 = jnp.float32
    Bsz, L, _ = u.shape
    uf = u.astype(f32).reshape(Bsz, L, N_SSM_GROUPS, SSM_GROUP)
    lam = lax.complex(a_re.astype(f32), a_im.astype(f32))
    step = jnp.exp(log_step.astype(f32))[:, None]
    lam_bar = jnp.exp(lam * step)
    b_tilde = lax.complex(b_re.astype(f32), b_im.astype(f32))
    b_bar = ((lam_bar - 1.0) / lam)[..., None] * b_tilde
    bu = jnp.einsum('gph,blgh->blgp', b_bar, uf.astype(jnp.complex64))
    a_elems = jnp.broadcast_to(lam_bar, bu.shape)

    def combine(left, right):
        a_l, b_l = left
        a_r, b_r = right
        return a_r * a_l, a_r * b_l + b_r

    _, states = lax.associative_scan(combine, (a_elems, bu), axis=1)
    c_tilde = lax.complex(c_re.astype(f32), c_im.astype(f32))
    y = jnp.real(jnp.einsum('ghp,blgp->blgh', c_tilde, states))
    y = y + d_skip.astype(f32).reshape(N_SSM_GROUPS, SSM_GROUP) * uf
    y = jax.nn.gelu(y.reshape(Bsz, L, D_SSM))
    y = y * jax.nn.sigmoid(y @ glu_w.astype(f32) + glu_b.astype(f32))
    return y.astype(u.dtype)


def pool_mixer(z, pool_w, pool_scale):
    f32 = jnp.float32
    Bsz, L, _ = z.shape
    zf = z.astype(f32).reshape(Bsz, L, len(POOL_WINDOWS), POOL_GROUP)
    csum = lax.cumsum(zf, axis=1)
    count = jnp.arange(1, L + 1, dtype=f32)[None, :, None]
    outs = []
    for g, w in enumerate(POOL_WINDOWS):
        s = csum[:, :, g]
        lower = jnp.pad(s, ((0, 0), (w, 0), (0, 0)))[:, :L]
        mean = (s - lower) / jnp.minimum(count, w)
        outs.append(mean - zf[:, :, g])
    pooled = jnp.stack(outs, axis=2)
    y = jnp.einsum('blgc,gcd->blgd', pooled, pool_w.astype(f32)).reshape(Bsz, L, D_POOL)
    return (y * pool_scale.astype(f32)).astype(z.dtype)


def sgu_mixer(su, sv, norm_g, sgu_w, sgu_b):
    Bsz, L, _ = su.shape
    v = rmsnorm(sv, norm_g)
    vr = v.reshape(Bsz, L // CHUNK, CHUNK, SGU_HEADS, SGU_HEAD_DIM)
    mask = jnp.tril(jnp.ones((CHUNK, CHUNK), dtype=bool))
    w_s = jnp.where(mask, sgu_w, 0)
    mixed = jnp.einsum('hts,bnshd->bnthd', w_s, vr) + jnp.swapaxes(sgu_b, 0, 1)[:, :, None]
    return su * mixed.reshape(Bsz, L, D_SGU)


def conv_ffn(h, w_up, conv_w, conv_b, w_down):
    up = causal_dwconv(h @ w_up, conv_w) + conv_b
    g, v = jnp.split(up, 2, axis=-1)
    return (jax.nn.silu(g) * v) @ w_down


def setup_inputs(seed: int = 0) -> dict:
    key = jax.random.key(seed)
    ks = jax.random.split(key, 32)
    f32 = jnp.float32

    def nrm(k, shape, scale):
        return jax.random.normal(k, shape, f32) * scale

    G, P, Hg = N_SSM_GROUPS, SSM_STATE, SSM_GROUP
    d_even_in = 3 * D_CONV + D_SSM
    d_odd_in = D_POOL + 2 * D_SGU
    a_im_base = jnp.pi * jnp.arange(P, dtype=f32)
    return {
        "x": nrm(ks[0], (BATCH, SEQ, D_MODEL), 1.0),
        "norm_mix_g": 1.0 + nrm(ks[1], (DEPTH, D_MODEL), 0.05),
        "even_w_in": nrm(ks[2], (N_EVEN, D_MODEL, d_even_in), D_MODEL ** -0.5),
        "even_conv_w": nrm(ks[3], (N_EVEN, CONV_WIDTH, D_CONV), CONV_WIDTH ** -0.5),
        "ssm_log_step": jax.random.uniform(ks[4], (N_EVEN, G), f32, math.log(1e-3), math.log(1e-1)),
        "ssm_a_re": -0.5 * (1.0 + nrm(ks[5], (N_EVEN, G, P), 0.01)),
        "ssm_a_im": a_im_base + nrm(ks[6], (N_EVEN, G, P), 0.01),
        "ssm_b_re": nrm(ks[7], (N_EVEN, G, P, Hg), (2 * Hg) ** -0.5),
        "ssm_b_im": nrm(ks[8], (N_EVEN, G, P, Hg), (2 * Hg) ** -0.5),
        "ssm_c_re": nrm(ks[9], (N_EVEN, G, Hg, P), (2 * P) ** -0.5),
        "ssm_c_im": nrm(ks[10], (N_EVEN, G, Hg, P), (2 * P) ** -0.5),
        "ssm_d": nrm(ks[11], (N_EVEN, D_SSM), 1.0),
        "ssm_glu_w": nrm(ks[12], (N_EVEN, D_SSM, D_SSM), D_SSM ** -0.5),
        "ssm_glu_b": nrm(ks[13], (N_EVEN, D_SSM), 0.02),
        "even_w_out": nrm(ks[14], (N_EVEN, D_CONV + D_SSM, D_MODEL), (D_CONV + D_SSM) ** -0.5),
        "odd_w_in": nrm(ks[15], (N_ODD, D_MODEL, d_odd_in), D_MODEL ** -0.5),
        "pool_w": nrm(ks[16], (N_ODD, len(POOL_WINDOWS), POOL_GROUP, POOL_GROUP), POOL_GROUP ** -0.5),
        "pool_scale": 1.0 + nrm(ks[17], (N_ODD, D_POOL), 0.1),
        "sgu_norm_g": 1.0 + nrm(ks[18], (N_ODD, D_SGU), 0.05),
        "sgu_w": nrm(ks[19], (N_ODD, SGU_HEADS, CHUNK, CHUNK), CHUNK ** -0.5),
        "sgu_b": 1.0 + nrm(ks[20], (N_ODD, SGU_HEADS, CHUNK), 0.1),
        "odd_w_out": nrm(ks[21], (N_ODD, D_POOL + D_SGU, D_MODEL), (D_POOL + D_SGU) ** -0.5),
        "norm_ffn_g": 1.0 + nrm(ks[22], (DEPTH, D_MODEL), 0.05),
        "ffn_w_up": nrm(ks[23], (DEPTH, D_MODEL, 2 * D_FF), D_MODEL ** -0.5),
        "ffn_conv_w": nrm(ks[24], (DEPTH, CONV_WIDTH, 2 * D_FF), CONV_WIDTH ** -0.5),
        "ffn_conv_b": nrm(ks[25], (DEPTH, 2 * D_FF), 0.02),
        "ffn_w_down": nrm(ks[26], (DEPTH, D_FF, D_MODEL), D_FF ** -0.5),
        "norm_final_g": 1.0 + nrm(ks[27], (D_MODEL,), 0.05),
    }


def reference(x, norm_mix_g, even_w_in, even_conv_w, ssm_log_step, ssm_a_re, ssm_a_im,
              ssm_b_re, ssm_b_im, ssm_c_re, ssm_c_im, ssm_d, ssm_glu_w, ssm_glu_b,
              even_w_out, odd_w_in, pool_w, pool_scale, sgu_norm_g, sgu_w, sgu_b,
              odd_w_out, norm_ffn_g, ffn_w_up, ffn_conv_w, ffn_conv_b, ffn_w_down,
              norm_final_g):
    for i in range(DEPTH):
        h = rmsnorm(x, norm_mix_g[i])
        j = i // 2
        if i % 2 == 0:
            proj = h @ even_w_in[j]
            xa = proj[..., :D_CONV]
            ba = proj[..., D_CONV:2 * D_CONV]
            ca = proj[..., 2 * D_CONV:3 * D_CONV]
            u = proj[..., 3 * D_CONV:]
            ya = short_conv_mixer(xa, ba, ca, even_conv_w[j])
            yb = s5_mixer(u, ssm_log_step[j], ssm_a_re[j], ssm_a_im[j], ssm_b_re[j], ssm_b_im[j],
                          ssm_c_re[j], ssm_c_im[j], ssm_d[j], ssm_glu_w[j], ssm_glu_b[j])
            mix = jnp.concatenate([ya, yb], axis=-1) @ even_w_out[j]
        else:
            proj = h @ odd_w_in[j]
            z = proj[..., :D_POOL]
            uv = jax.nn.gelu(proj[..., D_POOL:])
            su = uv[..., :D_SGU]
            sv = uv[..., D_SGU:]
            yc = pool_mixer(z, pool_w[j], pool_scale[j])
            yd = sgu_mixer(su, sv, sgu_norm_g[j], sgu_w[j], sgu_b[j])
            mix = jnp.concatenate([yc, yd], axis=-1) @ odd_w_out[j]
        x = x + mix
        x = x + conv_ffn(rmsnorm(x, norm_ffn_g[i]), ffn_w_up[i], ffn_conv_w[i], ffn_conv_b[i], ffn_w_down[i])
    return rmsnorm(x, norm_final_g)
```

```python
import functools

import jax
import jax.numpy as jnp
from jax import lax
from jax.experimental import pallas as pl
from jax.experimental.pallas import tpu as pltpu

D_MODEL = 1024
BATCH = 8
SEQ = 4096
DEPTH = 4
D_CONV = D_MODEL // 2
CONV_WIDTH = 3
D_SSM = D_MODEL // 2
SSM_GROUP = 16
N_SSM_GROUPS = D_SSM // SSM_GROUP
SSM_STATE = 64
D_POOL = D_MODEL // 2
POOL_WINDOWS = (2, 4, 8, 16)
POOL_GROUP = D_POOL // len(POOL_WINDOWS)
D_SGU = D_MODEL // 2
SGU_HEADS = 4
SGU_HEAD_DIM = D_SGU // SGU_HEADS
CHUNK = 128
D_FF = ((8 * D_MODEL // 3 + 127) // 128) * 128
EPS = 1e-6

N_ROWS = SEQ * BATCH
V7X_VMEM_BYTES = 64 * 1024 * 1024
VMEM_LIMIT_BYTES = V7X_VMEM_BYTES - 8 * 1024 * 1024

CONV_HALO = (CONV_WIDTH - 1) * BATCH
POOL_HALO = max(POOL_WINDOWS) * BATCH
SSM_HALF = D_SSM // 2
SSM_HALF_STATES = N_SSM_GROUPS // 2 * SSM_STATE
SSM_COLS = 4 * SSM_HALF_STATES
SCAN_COLS = 512

EVEN_TQ = 64
FFN_TQ = 64
ODD_TQ = CHUNK

f32 = jnp.float32
bf16 = jnp.bfloat16


def _rms(x, g):
    return x * lax.rsqrt(jnp.mean(x * x, axis=-1, keepdims=True) + EPS) * g


def _dot(a, b):
    return jnp.dot(a, b, preferred_element_type=f32)


def _full(shape):
    return pl.BlockSpec(shape, lambda i: (0,) * len(shape))


def _rows(r, d):
    return pl.BlockSpec((r, d), lambda i: (i, 0))


def _params():
    return pltpu.CompilerParams(dimension_semantics=("arbitrary",),
                                vmem_limit_bytes=VMEM_LIMIT_BYTES)


def _s5_param_kernel(ls_ref, are_ref, aim_ref, bre_ref, bim_ref,
                     lre_ref, lim_ref, obre_ref, obim_ref):
    step = jnp.exp(ls_ref[...])
    a_re = are_ref[...]
    a_im = aim_ref[...]
    mag = jnp.exp(a_re * step)
    l_re = mag * jnp.cos(a_im * step)
    l_im = mag * jnp.sin(a_im * step)
    lre_ref[...] = l_re
    lim_ref[...] = l_im
    n_re = l_re - 1.0
    den = a_re * a_re + a_im * a_im
    c_re = (n_re * a_re + l_im * a_im) / den
    c_im = (l_im * a_re - n_re * a_im) / den
    b_re = bre_ref[...]
    b_im = bim_ref[...]
    obre_ref[...] = c_re * b_re - c_im * b_im
    obim_ref[...] = c_re * b_im + c_im * b_re


def _s5_params(log_step, a_re, a_im, b_re, b_im):
    gp = N_SSM_GROUPS * SSM_STATE
    ls = jnp.repeat(log_step, SSM_STATE).reshape(gp, 1)
    col = jax.ShapeDtypeStruct((gp, 1), f32)
    mat = jax.ShapeDtypeStruct((gp, SSM_GROUP), f32)
    return pl.pallas_call(
        _s5_param_kernel,
        out_shape=(col, col, mat, mat),
        name="s5_params",
    )(ls, a_re.reshape(gp, 1), a_im.reshape(gp, 1),
      b_re.reshape(gp, SSM_GROUP), b_im.reshape(gp, SSM_GROUP))


def _block_diag_halves(w):
    g, a, b = w.shape
    gh = g // 2
    w = w.reshape(2, gh, a, 1, b)
    eye = jnp.eye(gh, dtype=bool).reshape(1, gh, 1, gh, 1)
    return jnp.where(eye, w, 0.0).reshape(2, gh * a, gh * b)


def _even_kernel(x_ref, g_ref, win_ref, cw_ref, ar_ref, ai_ref, bm_ref, cm_ref,
                 dsk_ref, gw_ref, gb_ref, wout_ref, o_ref, cx_ref, bu_ref, st_ref):
    i = pl.program_id(0)
    rows = x_ref.shape[0]
    tq = rows // BATCH

    @pl.when(i == 0)
    def _():
        cx_ref[0:CONV_HALO, :] = jnp.zeros((CONV_HALO, D_CONV), f32)
        st_ref[...] = jnp.zeros_like(st_ref)

    x = x_ref[...]
    h = _rms(x, g_ref[...]).astype(bf16)
    proj = _dot(h, win_ref[...])
    xa = proj[:, 0:D_CONV]
    ba = proj[:, D_CONV:2 * D_CONV]
    ca = proj[:, 2 * D_CONV:3 * D_CONV]
    u = proj[:, 3 * D_CONV:]

    cx_ref[CONV_HALO:CONV_HALO + rows, :] = ca * xa
    conv = cx_ref[0:rows, :] * cw_ref[0:1, :]
    for k in range(1, CONV_WIDTH):
        conv = conv + cx_ref[k * BATCH:k * BATCH + rows, :] * cw_ref[k:k + 1, :]
    ya = ba * conv
    cx_ref[0:CONV_HALO, :] = cx_ref[rows:rows + CONV_HALO, :]

    ub = u.astype(bf16)
    for k in range(2):
        bu_ref[:, 2 * SSM_HALF_STATES * k:2 * SSM_HALF_STATES * (k + 1)] = _dot(
            ub[:, SSM_HALF * k:SSM_HALF * (k + 1)], bm_ref[k])

    for k in range(2):
        for cb in range(SSM_HALF_STATES // SCAN_COLS):
            c_re = 2 * SSM_HALF_STATES * k + cb * SCAN_COLS
            c_im = c_re + SSM_HALF_STATES
            a_col = SSM_HALF_STATES * k + cb * SCAN_COLS
            ar = jnp.broadcast_to(ar_ref[:, a_col:a_col + SCAN_COLS], (BATCH, SCAN_COLS))
            ai = jnp.broadcast_to(ai_ref[:, a_col:a_col + SCAN_COLS], (BATCH, SCAN_COLS))

            def body(t, carry, c_re=c_re, c_im=c_im, ar=ar, ai=ai):
                s_re, s_im = carry
                r0 = pl.multiple_of(t * BATCH, BATCH)
                n_re = ar * s_re - ai * s_im + bu_ref[pl.ds(r0, BATCH), c_re:c_re + SCAN_COLS]
                n_im = ar * s_im + ai * s_re + bu_ref[pl.ds(r0, BATCH), c_im:c_im + SCAN_COLS]
                bu_ref[pl.ds(r0, BATCH), c_re:c_re + SCAN_COLS] = n_re
                bu_ref[pl.ds(r0, BATCH), c_im:c_im + SCAN_COLS] = n_im
                return n_re, n_im

            s_re, s_im = lax.fori_loop(
                0, tq, body,
                (st_ref[:, c_re:c_re + SCAN_COLS], st_ref[:, c_im:c_im + SCAN_COLS]),
                unroll=8)
            st_ref[:, c_re:c_re + SCAN_COLS] = s_re
            st_ref[:, c_im:c_im + SCAN_COLS] = s_im

    ys = []
    for k in range(2):
        s = bu_ref[:, 2 * SSM_HALF_STATES * k:2 * SSM_HALF_STATES * (k + 1)].astype(bf16)
        ys.append(_dot(s, cm_ref[k]))
    y = jnp.concatenate(ys, axis=1) + dsk_ref[...] * u
    y = jax.nn.gelu(y)
    yb = y * jax.nn.sigmoid(_dot(y.astype(bf16), gw_ref[...]) + gb_ref[...])

    mix = _dot(ya.astype(bf16), wout_ref[0:D_CONV, :]) + _dot(yb.astype(bf16), wout_ref[D_CONV:, :])
    o_ref[...] = x + mix


def _even_mixer(x, g, w_in, conv_w, ar, ai, bmat, cmat, d_skip, glu_w, glu_b, w_out):
    rows = EVEN_TQ * BATCH
    args = (x, g, w_in, conv_w, ar, ai, bmat, cmat, d_skip, glu_w, glu_b, w_out)
    in_specs = [_rows(rows, D_MODEL)] + [_full(a.shape) for a in args[1:]]
    return pl.pallas_call(
        _even_kernel,
        grid=(N_ROWS // rows,),
        in_specs=in_specs,
        out_specs=_rows(rows, D_MODEL),
        out_shape=jax.ShapeDtypeStruct((N_ROWS, D_MODEL), f32),
        scratch_shapes=[
            pltpu.VMEM((rows + CONV_HALO, D_CONV), f32),
            pltpu.VMEM((rows, SSM_COLS), f32),
            pltpu.VMEM((BATCH, SSM_COLS), f32),
        ],
        compiler_params=_params(),
        name="even_mixer",
    )(*args)


def _sgu_weight_kernel(w_ref, o_ref):
    n = CHUNK * BATCH
    w = w_ref[0]
    tri = lax.broadcasted_iota(jnp.int32, (CHUNK, CHUNK), 0) >= lax.broadcasted_iota(
        jnp.int32, (CHUNK, CHUNK), 1)
    wm = jnp.where(tri, w, 0.0).astype(bf16)
    expand = (lax.broadcasted_iota(jnp.int32, (n, CHUNK), 0) // BATCH
              == lax.broadcasted_iota(jnp.int32, (n, CHUNK), 1)).astype(bf16)
    expand_t = (lax.broadcasted_iota(jnp.int32, (CHUNK, n), 1) // BATCH
                == lax.broadcasted_iota(jnp.int32, (CHUNK, n), 0)).astype(bf16)
    rep = _dot(_dot(expand, wm).astype(bf16), expand_t)
    same_b = (lax.broadcasted_iota(jnp.int32, (n, n), 0) % BATCH
              == lax.broadcasted_iota(jnp.int32, (n, n), 1) % BATCH)
    o_ref[0] = jnp.where(same_b, rep, 0.0).astype(bf16)


def _sgu_weights(sgu_w):
    n = CHUNK * BATCH
    return pl.pallas_call(
        _sgu_weight_kernel,
        grid=(SGU_HEADS,),
        in_specs=[pl.BlockSpec((1, CHUNK, CHUNK), lambda i: (i, 0, 0))],
        out_specs=pl.BlockSpec((1, n, n), lambda i: (i, 0, 0)),
        out_shape=jax.ShapeDtypeStruct((SGU_HEADS, n, n), bf16),
        compiler_params=_params(),
        name="sgu_weights",
    )(sgu_w)


def _odd_kernel(x_ref, g_ref, win_ref, pw_ref, ps_ref, ng_ref, wk_ref, sb_ref, wout_ref,
                o_ref, z_ref):
    i = pl.program_id(0)
    rows = x_ref.shape[0]
    tq = rows // BATCH

    @pl.when(i == 0)
    def _():
        z_ref[0:POOL_HALO, :] = jnp.zeros((POOL_HALO, D_POOL), f32)

    x = x_ref[...]
    h = _rms(x, g_ref[...]).astype(bf16)
    proj = _dot(h, win_ref[...])
    z = proj[:, 0:D_POOL]
    uv = jax.nn.gelu(proj[:, D_POOL:])
    su = uv[:, 0:D_SGU]
    sv = uv[:, D_SGU:]

    z_ref[POOL_HALO:POOL_HALO + rows, :] = z
    count = (i * tq + 1 + lax.broadcasted_iota(jnp.int32, (rows, 1), 0) // BATCH).astype(f32)
    yc = []
    for gi, win in enumerate(POOL_WINDOWS):
        cols = slice(gi * POOL_GROUP, (gi + 1) * POOL_GROUP)
        e = z_ref[:, cols]
        span = 1
        while span < win:
            sh = span * BATCH
            e = e[sh:, :] + e[:-sh, :]
            span *= 2
        wsum = e[e.shape[0] - rows:, :]
        pooled = wsum / jnp.minimum(count, float(win)) - z[:, cols]
        yc.append(_dot(pooled.astype(bf16), pw_ref[gi]))
    yc = jnp.concatenate(yc, axis=1) * ps_ref[...]
    z_ref[0:POOL_HALO, :] = z_ref[rows:rows + POOL_HALO, :]

    v = _rms(sv, ng_ref[...]).astype(bf16)
    mixed = []
    for hd in range(SGU_HEADS):
        cols = slice(hd * SGU_HEAD_DIM, (hd + 1) * SGU_HEAD_DIM)
        mixed.append(_dot(wk_ref[hd], v[:, cols]) + sb_ref[:, hd:hd + 1])
    yd = su * jnp.concatenate(mixed, axis=1)

    mix = _dot(yc.astype(bf16), wout_ref[0:D_POOL, :]) + _dot(yd.astype(bf16), wout_ref[D_POOL:, :])
    o_ref[...] = x + mix


def _odd_mixer(x, g, w_in, pool_w, pool_scale, norm_g, wk, sgu_b_rows, w_out):
    rows = ODD_TQ * BATCH
    args = (x, g, w_in, pool_w, pool_scale, norm_g, wk, sgu_b_rows, w_out)
    in_specs = [_rows(rows, D_MODEL)] + [_full(a.shape) for a in args[1:]]
    return pl.pallas_call(
        _odd_kernel,
        grid=(N_ROWS // rows,),
        in_specs=in_specs,
        out_specs=_rows(rows, D_MODEL),
        out_shape=jax.ShapeDtypeStruct((N_ROWS, D_MODEL), f32),
        scratch_shapes=[pltpu.VMEM((rows + POOL_HALO, D_POOL), f32)],
        compiler_params=_params(),
        name="odd_mixer",
    )(*args)


def _ffn_chunks():
    out, c0 = [], 0
    while c0 < D_FF:
        cw = min(512, D_FF - c0)
        out.append((c0, cw))
        c0 += cw
    return out


def _ffn_kernel(x_ref, g_ref, wup_ref, cw_ref, cb_ref, wdn_ref, gf_ref, o_ref, up_ref, *, final):
    i = pl.program_id(0)
    rows = x_ref.shape[0]

    @pl.when(i == 0)
    def _():
        up_ref[0:CONV_HALO, :] = jnp.zeros((CONV_HALO, 2 * D_FF), f32)

    x = x_ref[...]
    h = _rms(x, g_ref[...]).astype(bf16)

    def conv(off, cw):
        cols = slice(off, off + cw)
        up_ref[CONV_HALO:CONV_HALO + rows, cols] = _dot(h, wup_ref[:, cols])
        y = up_ref[0:rows, cols] * cw_ref[0:1, cols]
        for k in range(1, CONV_WIDTH):
            y = y + up_ref[k * BATCH:k * BATCH + rows, cols] * cw_ref[k:k + 1, cols]
        return y + cb_ref[:, cols]

    acc = None
    for c0, cw in _ffn_chunks():
        gate = conv(c0, cw)
        val = conv(D_FF + c0, cw)
        act = (jax.nn.silu(gate) * val).astype(bf16)
        part = _dot(act, wdn_ref[c0:c0 + cw, :])
        acc = part if acc is None else acc + part
    up_ref[0:CONV_HALO, :] = up_ref[rows:rows + CONV_HALO, :]

    out = x + acc
    if final:
        out = _rms(out, gf_ref[...])
    o_ref[...] = out


def _conv_ffn(x, g, w_up, conv_w, conv_b, w_down, g_final, final):
    rows = FFN_TQ * BATCH
    args = (x, g, w_up, conv_w, conv_b, w_down, g_final)
    in_specs = [_rows(rows, D_MODEL)] + [_full(a.shape) for a in args[1:]]
    return pl.pallas_call(
        functools.partial(_ffn_kernel, final=final),
        grid=(N_ROWS // rows,),
        in_specs=in_specs,
        out_specs=_rows(rows, D_MODEL),
        out_shape=jax.ShapeDtypeStruct((N_ROWS, D_MODEL), f32),
        scratch_shapes=[pltpu.VMEM((rows + CONV_HALO, 2 * D_FF), f32)],
        compiler_params=_params(),
        name="conv_ffn",
    )(*args)


def kernel(x, norm_mix_g, even_w_in, even_conv_w, ssm_log_step, ssm_a_re, ssm_a_im, ssm_b_re, ssm_b_im, ssm_c_re, ssm_c_im, ssm_d, ssm_glu_w, ssm_glu_b, even_w_out, odd_w_in, pool_w, pool_scale, sgu_norm_g, sgu_w, sgu_b, odd_w_out, norm_ffn_g, ffn_w_up, ffn_conv_w, ffn_conv_b, ffn_w_down, norm_final_g):
    assert x.shape == (BATCH, SEQ, D_MODEL)
    xt = jnp.transpose(x, (1, 0, 2)).reshape(N_ROWS, D_MODEL)
    row = lambda v: v.reshape(1, -1)

    for i in range(DEPTH):
        j = i // 2
        if i % 2 == 0:
            l_re, l_im, bb_re, bb_im = _s5_params(
                ssm_log_step[j], ssm_a_re[j], ssm_a_im[j], ssm_b_re[j], ssm_b_im[j])
            shape_gph = (N_SSM_GROUPS, SSM_STATE, SSM_GROUP)
            bmat = jnp.concatenate(
                [_block_diag_halves(jnp.swapaxes(bb_re.reshape(shape_gph), 1, 2)),
                 _block_diag_halves(jnp.swapaxes(bb_im.reshape(shape_gph), 1, 2))],
                axis=2).astype(bf16)
            cmat = jnp.concatenate(
                [_block_diag_halves(jnp.swapaxes(ssm_c_re[j], 1, 2)),
                 _block_diag_halves(-jnp.swapaxes(ssm_c_im[j], 1, 2))],
                axis=1).astype(bf16)
            xt = _even_mixer(
                xt, row(norm_mix_g[i]), even_w_in[j].astype(bf16), even_conv_w[j],
                row(l_re), row(l_im), bmat, cmat, row(ssm_d[j]),
                ssm_glu_w[j].astype(bf16), row(ssm_glu_b[j]), even_w_out[j].astype(bf16))
        else:
            wk = _sgu_weights(sgu_w[j])
            sgu_b_rows = jnp.repeat(sgu_b[j].T, BATCH, axis=0)
            xt = _odd_mixer(
                xt, row(norm_mix_g[i]), odd_w_in[j].astype(bf16), pool_w[j].astype(bf16),
                row(pool_scale[j]), row(sgu_norm_g[j]), wk, sgu_b_rows,
                odd_w_out[j].astype(bf16))
        xt = _conv_ffn(
            xt, row(norm_ffn_g[i]), ffn_w_up[i].astype(bf16), ffn_conv_w[i],
            row(ffn_conv_b[i]), ffn_w_down[i].astype(bf16), row(norm_final_g),
            final=(i == DEPTH - 1))

    return jnp.transpose(xt.reshape(SEQ, BATCH, D_MODEL), (1, 0, 2))
```

```python
import jax
import jax.numpy as jnp
from jax import lax
from jax.experimental import pallas as pl
from jax.experimental.pallas import tpu as pltpu

D_MODEL = 1024
BATCH = 8
SEQ = 4096
DEPTH = 4
D_CONV = D_MODEL // 2
CONV_WIDTH = 3
D_SSM = D_MODEL // 2
SSM_GROUP = 16
N_SSM_GROUPS = D_SSM // SSM_GROUP
SSM_STATE = 64
D_POOL = D_MODEL // 2
POOL_WINDOWS = (2, 4, 8, 16)
POOL_GROUP = D_POOL // len(POOL_WINDOWS)
D_SGU = D_MODEL // 2
SGU_HEADS = 4
SGU_HEAD_DIM = D_SGU // SGU_HEADS
CHUNK = 128
D_FF = ((8 * D_MODEL // 3 + 127) // 128) * 128
EPS = 1e-6

N_ROWS = SEQ * BATCH
LANES = 128
V7X_VMEM_BYTES = 64 * 1024 * 1024
VMEM_LIMIT_BYTES = V7X_VMEM_BYTES - 8 * 1024 * 1024

CONV_HALO = (CONV_WIDTH - 1) * BATCH
POOL_HALO = max(POOL_WINDOWS) * BATCH
SSM_HALF = D_SSM // 2
SSM_HALF_STATES = N_SSM_GROUPS // 2 * SSM_STATE
SSM_COLS = 4 * SSM_HALF_STATES
SCAN_COLS = 512

EVEN_TQ = 64
FFN_TQ = 64
ODD_TQ = CHUNK

f32 = jnp.float32
bf16 = jnp.bfloat16


def _rms(x, g):
    return x * lax.rsqrt(jnp.mean(x * x, axis=-1, keepdims=True) + EPS) * g


def _dot(a, b):
    return jnp.dot(a, b, preferred_element_type=f32)


def _layer_spec(arr, j):
    nd = arr.ndim
    return pl.BlockSpec((None,) + arr.shape[1:], lambda i: (j,) + (0,) * (nd - 1),
                        pipeline_mode=pl.Buffered(1))


def _rows(r, d):
    return pl.BlockSpec((r, d), lambda i: (i, 0))


def _batch_rows(tq, d):
    return pl.BlockSpec((BATCH, tq, d), lambda i: (0, i, 0))


def _params():
    return pltpu.CompilerParams(dimension_semantics=("arbitrary",),
                                vmem_limit_bytes=VMEM_LIMIT_BYTES)


def _s5_param_kernel(ls_ref, are_ref, aim_ref, bre_ref, bim_ref,
                     lre_ref, lim_ref, obre_ref, obim_ref):
    step = jnp.exp(ls_ref[...])
    a_re = are_ref[...]
    a_im = aim_ref[...]
    mag = jnp.exp(a_re * step)
    l_re = mag * jnp.cos(a_im * step)
    l_im = mag * jnp.sin(a_im * step)
    lre_ref[...] = l_re
    lim_ref[...] = l_im
    n_re = l_re - 1.0
    den = a_re * a_re + a_im * a_im
    c_re = (n_re * a_re + l_im * a_im) / den
    c_im = (l_im * a_re - n_re * a_im) / den
    b_re = bre_ref[...]
    b_im = bim_ref[...]
    obre_ref[...] = c_re * b_re - c_im * b_im
    obim_ref[...] = c_re * b_im + c_im * b_re


def _s5_params(log_step, a_re, a_im, b_re, b_im):
    gp = log_step.shape[0] * N_SSM_GROUPS * SSM_STATE
    ls = jnp.repeat(log_step.reshape(-1), SSM_STATE).reshape(gp, 1)
    col = jax.ShapeDtypeStruct((gp, 1), f32)
    mat = jax.ShapeDtypeStruct((gp, SSM_GROUP), f32)
    return pl.pallas_call(
        _s5_param_kernel,
        out_shape=(col, col, mat, mat),
        name="s5_params",
    )(ls, a_re.reshape(gp, 1), a_im.reshape(gp, 1),
      b_re.reshape(gp, SSM_GROUP), b_im.reshape(gp, SSM_GROUP))


def _block_diag_halves(w):
    *lead, g, a, b = w.shape
    gh = g // 2
    w = w.reshape(*lead, 2, gh, a, 1, b)
    eye = jnp.eye(gh, dtype=bool).reshape(gh, 1, gh, 1)
    return jnp.where(eye, w, 0.0).reshape(*lead, 2, gh * a, gh * b)


def _even_kernel(x_ref, g_ref, win_ref, cw_ref, ar_ref, ai_ref, bm_ref, cm_ref,
                 dsk_ref, gw_ref, gb_ref, wout_ref, o_ref, cx_ref, bu_ref, st_ref,
                 xt_ref=None):
    i = pl.program_id(0)
    rows = o_ref.shape[0]
    tq = rows // BATCH

    @pl.when(i == 0)
    def _():
        cx_ref[0:CONV_HALO, :] = jnp.zeros((CONV_HALO, D_CONV), f32)
        st_ref[...] = jnp.zeros_like(st_ref)

    if xt_ref is None:
        x = x_ref[...]
    else:
        for b in range(BATCH):
            for c in range(D_MODEL // LANES):
                xt_ref[c, pl.ds(b, tq, stride=BATCH), :] = x_ref[b, :, c * LANES:(c + 1) * LANES]
        x = jnp.concatenate([xt_ref[c] for c in range(D_MODEL // LANES)], axis=1)
    h = _rms(x, g_ref[...]).astype(bf16)
    proj = _dot(h, win_ref[...])
    xa = proj[:, 0:D_CONV]
    ba = proj[:, D_CONV:2 * D_CONV]
    ca = proj[:, 2 * D_CONV:3 * D_CONV]
    u = proj[:, 3 * D_CONV:]

    cx_ref[CONV_HALO:CONV_HALO + rows, :] = ca * xa
    conv = cx_ref[0:rows, :] * cw_ref[0:1, :]
    for k in range(1, CONV_WIDTH):
        conv = conv + cx_ref[k * BATCH:k * BATCH + rows, :] * cw_ref[k:k + 1, :]
    ya = ba * conv
    cx_ref[0:CONV_HALO, :] = cx_ref[rows:rows + CONV_HALO, :]

    ub = u.astype(bf16)
    for k in range(2):
        bu_ref[:, 2 * SSM_HALF_STATES * k:2 * SSM_HALF_STATES * (k + 1)] = _dot(
            ub[:, SSM_HALF * k:SSM_HALF * (k + 1)], bm_ref[k])

    for k in range(2):
        for cb in range(SSM_HALF_STATES // SCAN_COLS):
            c_re = 2 * SSM_HALF_STATES * k + cb * SCAN_COLS
            c_im = c_re + SSM_HALF_STATES
            a_col = SSM_HALF_STATES * k + cb * SCAN_COLS
            ar = jnp.broadcast_to(ar_ref[:, a_col:a_col + SCAN_COLS], (BATCH, SCAN_COLS))
            ai = jnp.broadcast_to(ai_ref[:, a_col:a_col + SCAN_COLS], (BATCH, SCAN_COLS))

            def body(t, carry, c_re=c_re, c_im=c_im, ar=ar, ai=ai):
                s_re, s_im = carry
                r0 = pl.multiple_of(t * BATCH, BATCH)
                n_re = ar * s_re - ai * s_im + bu_ref[pl.ds(r0, BATCH), c_re:c_re + SCAN_COLS]
                n_im = ar * s_im + ai * s_re + bu_ref[pl.ds(r0, BATCH), c_im:c_im + SCAN_COLS]
                bu_ref[pl.ds(r0, BATCH), c_re:c_re + SCAN_COLS] = n_re
                bu_ref[pl.ds(r0, BATCH), c_im:c_im + SCAN_COLS] = n_im
                return n_re, n_im

            s_re, s_im = lax.fori_loop(
                0, tq, body,
                (st_ref[:, c_re:c_re + SCAN_COLS], st_ref[:, c_im:c_im + SCAN_COLS]),
                unroll=8)
            st_ref[:, c_re:c_re + SCAN_COLS] = s_re
            st_ref[:, c_im:c_im + SCAN_COLS] = s_im

    ys = []
    for k in range(2):
        s = bu_ref[:, 2 * SSM_HALF_STATES * k:2 * SSM_HALF_STATES * (k + 1)].astype(bf16)
        ys.append(_dot(s, cm_ref[k]))
    y = jnp.concatenate(ys, axis=1) + dsk_ref[...] * u
    y = jax.nn.gelu(y)
    yb = y * jax.nn.sigmoid(_dot(y.astype(bf16), gw_ref[...]) + gb_ref[...])

    mix = _dot(ya.astype(bf16), wout_ref[0:D_CONV, :]) + _dot(yb.astype(bf16), wout_ref[D_CONV:, :])
    o_ref[...] = x + mix


def _even_mixer(x, j, layer_params, batch_major_in):
    rows = EVEN_TQ * BATCH
    x_spec = _batch_rows(EVEN_TQ, D_MODEL) if batch_major_in else _rows(rows, D_MODEL)
    scratch = [
        pltpu.VMEM((rows + CONV_HALO, D_CONV), f32),
        pltpu.VMEM((rows, SSM_COLS), f32),
        pltpu.VMEM((BATCH, SSM_COLS), f32),
    ]
    if batch_major_in:
        scratch.append(pltpu.VMEM((D_MODEL // LANES, rows, LANES), f32))
    return pl.pallas_call(
        _even_kernel,
        grid=(N_ROWS // rows,),
        in_specs=[x_spec] + [_layer_spec(a, j) for a in layer_params],
        out_specs=_rows(rows, D_MODEL),
        out_shape=jax.ShapeDtypeStruct((N_ROWS, D_MODEL), f32),
        scratch_shapes=scratch,
        compiler_params=_params(),
        name="even_mixer",
    )(x, *layer_params)


def _sgu_weight_kernel(w_ref, o_ref):
    n = CHUNK * BATCH
    w = w_ref[0]
    tri = lax.broadcasted_iota(jnp.int32, (CHUNK, CHUNK), 0) >= lax.broadcasted_iota(
        jnp.int32, (CHUNK, CHUNK), 1)
    wm = jnp.where(tri, w, 0.0).astype(bf16)
    expand = (lax.broadcasted_iota(jnp.int32, (n, CHUNK), 0) // BATCH
              == lax.broadcasted_iota(jnp.int32, (n, CHUNK), 1)).astype(bf16)
    expand_t = (lax.broadcasted_iota(jnp.int32, (CHUNK, n), 1) // BATCH
                == lax.broadcasted_iota(jnp.int32, (CHUNK, n), 0)).astype(bf16)
    rep = _dot(_dot(expand, wm).astype(bf16), expand_t)
    same_b = (lax.broadcasted_iota(jnp.int32, (n, n), 0) % BATCH
              == lax.broadcasted_iota(jnp.int32, (n, n), 1) % BATCH)
    o_ref[0] = jnp.where(same_b, rep, 0.0).astype(bf16)


def _sgu_weights(sgu_w):
    n = CHUNK * BATCH
    lead = sgu_w.shape[:2]
    nw = lead[0] * lead[1]
    out = pl.pallas_call(
        _sgu_weight_kernel,
        grid=(nw,),
        in_specs=[pl.BlockSpec((1, CHUNK, CHUNK), lambda i: (i, 0, 0))],
        out_specs=pl.BlockSpec((1, n, n), lambda i: (i, 0, 0)),
        out_shape=jax.ShapeDtypeStruct((nw, n, n), bf16),
        compiler_params=_params(),
        name="sgu_weights",
    )(sgu_w.reshape(nw, CHUNK, CHUNK))
    return out.reshape(*lead, n, n)


def _odd_kernel(x_ref, g_ref, win_ref, pw_ref, ps_ref, ng_ref, wk_ref, sb_ref, wout_ref,
                o_ref, z_ref):
    i = pl.program_id(0)
    rows = x_ref.shape[0]
    tq = rows // BATCH

    @pl.when(i == 0)
    def _():
        z_ref[0:POOL_HALO, :] = jnp.zeros((POOL_HALO, D_POOL), f32)

    x = x_ref[...]
    h = _rms(x, g_ref[...]).astype(bf16)
    proj = _dot(h, win_ref[...])
    z = proj[:, 0:D_POOL]
    uv = jax.nn.gelu(proj[:, D_POOL:])
    su = uv[:, 0:D_SGU]
    sv = uv[:, D_SGU:]

    z_ref[POOL_HALO:POOL_HALO + rows, :] = z
    count = (i * tq + 1 + lax.broadcasted_iota(jnp.int32, (rows, 1), 0) // BATCH).astype(f32)
    yc = []
    for gi, win in enumerate(POOL_WINDOWS):
        cols = slice(gi * POOL_GROUP, (gi + 1) * POOL_GROUP)
        e = z_ref[:, cols]
        span = 1
        while span < win:
            sh = span * BATCH
            e = e[sh:, :] + e[:-sh, :]
            span *= 2
        wsum = e[e.shape[0] - rows:, :]
        pooled = wsum / jnp.minimum(count, float(win)) - z[:, cols]
        yc.append(_dot(pooled.astype(bf16), pw_ref[gi]))
    yc = jnp.concatenate(yc, axis=1) * ps_ref[...]
    z_ref[0:POOL_HALO, :] = z_ref[rows:rows + POOL_HALO, :]

    v = _rms(sv, ng_ref[...]).astype(bf16)
    mixed = []
    for hd in range(SGU_HEADS):
        cols = slice(hd * SGU_HEAD_DIM, (hd + 1) * SGU_HEAD_DIM)
        mixed.append(_dot(wk_ref[hd], v[:, cols]) + sb_ref[:, hd:hd + 1])
    yd = su * jnp.concatenate(mixed, axis=1)

    mix = _dot(yc.astype(bf16), wout_ref[0:D_POOL, :]) + _dot(yd.astype(bf16), wout_ref[D_POOL:, :])
    o_ref[...] = x + mix


def _odd_mixer(x, j, layer_params):
    rows = ODD_TQ * BATCH
    return pl.pallas_call(
        _odd_kernel,
        grid=(N_ROWS // rows,),
        in_specs=[_rows(rows, D_MODEL)] + [_layer_spec(a, j) for a in layer_params],
        out_specs=_rows(rows, D_MODEL),
        out_shape=jax.ShapeDtypeStruct((N_ROWS, D_MODEL), f32),
        scratch_shapes=[pltpu.VMEM((rows + POOL_HALO, D_POOL), f32)],
        compiler_params=_params(),
        name="odd_mixer",
    )(x, *layer_params)


def _ffn_chunks():
    out, c0 = [], 0
    while c0 < D_FF:
        cw = min(512, D_FF - c0)
        out.append((c0, cw))
        c0 += cw
    return out


def _ffn_kernel(x_ref, g_ref, wup_ref, cw_ref, cb_ref, wdn_ref, gf_ref, o_ref, up_ref,
                yt_ref=None):
    i = pl.program_id(0)
    rows = x_ref.shape[0]
    tq = rows // BATCH

    @pl.when(i == 0)
    def _():
        up_ref[0:CONV_HALO, :] = jnp.zeros((CONV_HALO, 2 * D_FF), f32)

    x = x_ref[...]
    h = _rms(x, g_ref[...]).astype(bf16)

    def conv(off, cw):
        cols = slice(off, off + cw)
        up_ref[CONV_HALO:CONV_HALO + rows, cols] = _dot(h, wup_ref[:, cols])
        y = up_ref[0:rows, cols] * cw_ref[0:1, cols]
        for k in range(1, CONV_WIDTH):
            y = y + up_ref[k * BATCH:k * BATCH + rows, cols] * cw_ref[k:k + 1, cols]
        return y + cb_ref[:, cols]

    acc = None
    for c0, cw in _ffn_chunks():
        gate = conv(c0, cw)
        val = conv(D_FF + c0, cw)
        act = (jax.nn.silu(gate) * val).astype(bf16)
        part = _dot(act, wdn_ref[c0:c0 + cw, :])
        acc = part if acc is None else acc + part
    up_ref[0:CONV_HALO, :] = up_ref[rows:rows + CONV_HALO, :]

    out = x + acc
    if yt_ref is None:
        o_ref[...] = out
    else:
        out = _rms(out, gf_ref[...])
        for c in range(D_MODEL // LANES):
            yt_ref[c] = out[:, c * LANES:(c + 1) * LANES]
        for b in range(BATCH):
            for c in range(D_MODEL // LANES):
                o_ref[b, :, c * LANES:(c + 1) * LANES] = yt_ref[c, pl.ds(b, tq, stride=BATCH), :]


def _conv_ffn(x, j, layer_params, g_final, final):
    rows = FFN_TQ * BATCH
    scratch = [pltpu.VMEM((rows + CONV_HALO, 2 * D_FF), f32)]
    if final:
        scratch.append(pltpu.VMEM((D_MODEL // LANES, rows, LANES), f32))
        out_spec = _batch_rows(FFN_TQ, D_MODEL)
        out_shape = jax.ShapeDtypeStruct((BATCH, SEQ, D_MODEL), f32)
    else:
        out_spec = _rows(rows, D_MODEL)
        out_shape = jax.ShapeDtypeStruct((N_ROWS, D_MODEL), f32)
    return pl.pallas_call(
        _ffn_kernel,
        grid=(N_ROWS // rows,),
        in_specs=([_rows(rows, D_MODEL)] + [_layer_spec(a, j) for a in layer_params]
                  + [_layer_spec(g_final, 0)]),
        out_specs=out_spec,
        out_shape=out_shape,
        scratch_shapes=scratch,
        compiler_params=_params(),
        name="conv_ffn",
    )(x, *layer_params, g_final)


def kernel(x, norm_mix_g, even_w_in, even_conv_w, ssm_log_step, ssm_a_re, ssm_a_im, ssm_b_re, ssm_b_im, ssm_c_re, ssm_c_im, ssm_d, ssm_glu_w, ssm_glu_b, even_w_out, odd_w_in, pool_w, pool_scale, sgu_norm_g, sgu_w, sgu_b, odd_w_out, norm_ffn_g, ffn_w_up, ffn_conv_w, ffn_conv_b, ffn_w_down, norm_final_g):
    assert x.shape == (BATCH, SEQ, D_MODEL)
    n_even, n_odd = even_w_in.shape[0], odd_w_in.shape[0]
    rows3 = lambda v: v.reshape(v.shape[0], 1, -1)

    l_re, l_im, bb_re, bb_im = _s5_params(ssm_log_step, ssm_a_re, ssm_a_im, ssm_b_re, ssm_b_im)
    shape_gph = (n_even, N_SSM_GROUPS, SSM_STATE, SSM_GROUP)
    bmat = jnp.concatenate(
        [_block_diag_halves(jnp.swapaxes(bb_re.reshape(shape_gph), 2, 3)),
         _block_diag_halves(jnp.swapaxes(bb_im.reshape(shape_gph), 2, 3))],
        axis=3).astype(bf16)
    cmat = jnp.concatenate(
        [_block_diag_halves(jnp.swapaxes(ssm_c_re, 2, 3)),
         _block_diag_halves(-jnp.swapaxes(ssm_c_im, 2, 3))],
        axis=2).astype(bf16)
    even_params = (
        rows3(norm_mix_g[0::2]), even_w_in.astype(bf16), even_conv_w,
        l_re.reshape(n_even, 1, -1), l_im.reshape(n_even, 1, -1), bmat, cmat, rows3(ssm_d),
        ssm_glu_w.astype(bf16), rows3(ssm_glu_b), even_w_out.astype(bf16))

    sgu_b_rows = jnp.repeat(jnp.swapaxes(sgu_b, 1, 2), BATCH, axis=1)
    odd_params = (
        rows3(norm_mix_g[1::2]), odd_w_in.astype(bf16), pool_w.astype(bf16),
        rows3(pool_scale), rows3(sgu_norm_g), _sgu_weights(sgu_w), sgu_b_rows,
        odd_w_out.astype(bf16))

    ffn_params = (
        rows3(norm_ffn_g), ffn_w_up.astype(bf16), ffn_conv_w, rows3(ffn_conv_b),
        ffn_w_down.astype(bf16))
    g_final = norm_final_g.reshape(1, 1, D_MODEL)

    xt = x
    for i in range(DEPTH):
        j = i // 2
        if i % 2 == 0:
            xt = _even_mixer(xt, j, even_params, batch_major_in=(i == 0))
        else:
            xt = _odd_mixer(xt, j, odd_params)
        xt = _conv_ffn(xt, i, ffn_params, g_final, final=(i == DEPTH - 1))
    return xt
```

```python
import jax
import jax.numpy as jnp
from jax import lax
from jax.experimental import pallas as pl
from jax.experimental.pallas import tpu as pltpu

D_MODEL = 1024
BATCH = 8
SEQ = 4096
DEPTH = 4
D_CONV = D_MODEL // 2
CONV_WIDTH = 3
D_SSM = D_MODEL // 2
SSM_GROUP = 16
N_SSM_GROUPS = D_SSM // SSM_GROUP
SSM_STATE = 64
D_POOL = D_MODEL // 2
POOL_WINDOWS = (2, 4, 8, 16)
POOL_GROUP = D_POOL // len(POOL_WINDOWS)
D_SGU = D_MODEL // 2
SGU_HEADS = 4
SGU_HEAD_DIM = D_SGU // SGU_HEADS
CHUNK = 128
D_FF = ((8 * D_MODEL // 3 + 127) // 128) * 128
EPS = 1e-6

N_ROWS = SEQ * BATCH
LANES = 128
V7X_VMEM_BYTES = 64 * 1024 * 1024
VMEM_LIMIT_BYTES = V7X_VMEM_BYTES - 8 * 1024 * 1024

CONV_HALO = (CONV_WIDTH - 1) * BATCH
POOL_HALO = max(POOL_WINDOWS) * BATCH
SSM_HALF = D_SSM // 2
SSM_HALF_STATES = N_SSM_GROUPS // 2 * SSM_STATE
SSM_COLS = 4 * SSM_HALF_STATES
SCAN_COLS = 512

EVEN_TQ = 64
FFN_TQ = 128
ODD_TQ = CHUNK

f32 = jnp.float32
bf16 = jnp.bfloat16


def _rms(x, g):
    return x * lax.rsqrt(jnp.mean(x * x, axis=-1, keepdims=True) + EPS) * g


def _dot(a, b):
    return jnp.dot(a, b, preferred_element_type=f32)


def _layer_spec(arr, j):
    nd = arr.ndim
    return pl.BlockSpec((None,) + arr.shape[1:], lambda i: (j,) + (0,) * (nd - 1),
                        pipeline_mode=pl.Buffered(1))


def _rows(r, d):
    return pl.BlockSpec((r, d), lambda i: (i, 0))


def _batch_rows(tq, d):
    return pl.BlockSpec((BATCH, tq, d), lambda i: (0, i, 0))


def _params():
    return pltpu.CompilerParams(dimension_semantics=("arbitrary",),
                                vmem_limit_bytes=VMEM_LIMIT_BYTES)


def _s5_param_kernel(ls_ref, are_ref, aim_ref, bre_ref, bim_ref,
                     lre_ref, lim_ref, obre_ref, obim_ref):
    step = jnp.exp(ls_ref[...])
    a_re = are_ref[...]
    a_im = aim_ref[...]
    mag = jnp.exp(a_re * step)
    l_re = mag * jnp.cos(a_im * step)
    l_im = mag * jnp.sin(a_im * step)
    lre_ref[...] = l_re
    lim_ref[...] = l_im
    n_re = l_re - 1.0
    den = a_re * a_re + a_im * a_im
    c_re = (n_re * a_re + l_im * a_im) / den
    c_im = (l_im * a_re - n_re * a_im) / den
    b_re = bre_ref[...]
    b_im = bim_ref[...]
    obre_ref[...] = c_re * b_re - c_im * b_im
    obim_ref[...] = c_re * b_im + c_im * b_re


def _s5_params(log_step, a_re, a_im, b_re, b_im):
    gp = log_step.shape[0] * N_SSM_GROUPS * SSM_STATE
    ls = jnp.repeat(log_step.reshape(-1), SSM_STATE).reshape(gp, 1)
    col = jax.ShapeDtypeStruct((gp, 1), f32)
    mat = jax.ShapeDtypeStruct((gp, SSM_GROUP), f32)
    return pl.pallas_call(
        _s5_param_kernel,
        out_shape=(col, col, mat, mat),
        name="s5_params",
    )(ls, a_re.reshape(gp, 1), a_im.reshape(gp, 1),
      b_re.reshape(gp, SSM_GROUP), b_im.reshape(gp, SSM_GROUP))


def _block_diag_halves(w):
    *lead, g, a, b = w.shape
    gh = g // 2
    w = w.reshape(*lead, 2, gh, a, 1, b)
    eye = jnp.eye(gh, dtype=bool).reshape(gh, 1, gh, 1)
    return jnp.where(eye, w, 0.0).reshape(*lead, 2, gh * a, gh * b)


def _even_kernel(x_ref, g_ref, win_ref, cw_ref, ar_ref, ai_ref, bm_ref, cm_ref,
                 dsk_ref, gw_ref, gb_ref, wout_ref, o_ref, cx_ref, bu_ref, st_ref,
                 xt_ref=None):
    i = pl.program_id(0)
    rows = o_ref.shape[0]
    tq = rows // BATCH

    @pl.when(i == 0)
    def _():
        cx_ref[0:CONV_HALO, :] = jnp.zeros((CONV_HALO, D_CONV), f32)
        st_ref[...] = jnp.zeros_like(st_ref)

    if xt_ref is None:
        x = x_ref[...]
    else:
        for b in range(BATCH):
            for c in range(D_MODEL // LANES):
                xt_ref[c, pl.ds(b, tq, stride=BATCH), :] = x_ref[b, :, c * LANES:(c + 1) * LANES]
        x = jnp.concatenate([xt_ref[c] for c in range(D_MODEL // LANES)], axis=1)
    h = _rms(x, g_ref[...]).astype(bf16)
    proj = _dot(h, win_ref[...])
    xa = proj[:, 0:D_CONV]
    ba = proj[:, D_CONV:2 * D_CONV]
    ca = proj[:, 2 * D_CONV:3 * D_CONV]
    u = proj[:, 3 * D_CONV:]

    cx_ref[CONV_HALO:CONV_HALO + rows, :] = ca * xa
    conv = cx_ref[0:rows, :] * cw_ref[0:1, :]
    for k in range(1, CONV_WIDTH):
        conv = conv + cx_ref[k * BATCH:k * BATCH + rows, :] * cw_ref[k:k + 1, :]
    ya = ba * conv
    cx_ref[0:CONV_HALO, :] = cx_ref[rows:rows + CONV_HALO, :]

    ub = u.astype(bf16)
    ys = []
    for k in range(2):
        uk = ub[:, SSM_HALF * k:SSM_HALF * (k + 1)]
        yk = None
        for cb in range(SSM_HALF_STATES // SCAN_COLS):
            c_re = 2 * SSM_HALF_STATES * k + cb * SCAN_COLS
            c_im = c_re + SSM_HALF_STATES
            w_re = cb * SCAN_COLS
            w_im = w_re + SSM_HALF_STATES
            a_col = SSM_HALF_STATES * k + cb * SCAN_COLS
            bu_ref[:, c_re:c_re + SCAN_COLS] = _dot(uk, bm_ref[k, :, w_re:w_re + SCAN_COLS])
            bu_ref[:, c_im:c_im + SCAN_COLS] = _dot(uk, bm_ref[k, :, w_im:w_im + SCAN_COLS])
            ar = jnp.broadcast_to(ar_ref[:, a_col:a_col + SCAN_COLS], (BATCH, SCAN_COLS))
            ai = jnp.broadcast_to(ai_ref[:, a_col:a_col + SCAN_COLS], (BATCH, SCAN_COLS))
            s_re = st_ref[:, c_re:c_re + SCAN_COLS]
            s_im = st_ref[:, c_im:c_im + SCAN_COLS]
            for t in range(tq):
                r0 = t * BATCH
                n_re = ar * s_re - ai * s_im + bu_ref[r0:r0 + BATCH, c_re:c_re + SCAN_COLS]
                n_im = ar * s_im + ai * s_re + bu_ref[r0:r0 + BATCH, c_im:c_im + SCAN_COLS]
                bu_ref[r0:r0 + BATCH, c_re:c_re + SCAN_COLS] = n_re
                bu_ref[r0:r0 + BATCH, c_im:c_im + SCAN_COLS] = n_im
                s_re, s_im = n_re, n_im
            st_ref[:, c_re:c_re + SCAN_COLS] = s_re
            st_ref[:, c_im:c_im + SCAN_COLS] = s_im
            part = (_dot(bu_ref[:, c_re:c_re + SCAN_COLS].astype(bf16), cm_ref[k, w_re:w_re + SCAN_COLS, :])
                    + _dot(bu_ref[:, c_im:c_im + SCAN_COLS].astype(bf16), cm_ref[k, w_im:w_im + SCAN_COLS, :]))
            yk = part if yk is None else yk + part
        ys.append(yk)
    y = jnp.concatenate(ys, axis=1) + dsk_ref[...] * u
    y = jax.nn.gelu(y)
    yb = y * jax.nn.sigmoid(_dot(y.astype(bf16), gw_ref[...]) + gb_ref[...])

    mix = _dot(ya.astype(bf16), wout_ref[0:D_CONV, :]) + _dot(yb.astype(bf16), wout_ref[D_CONV:, :])
    o_ref[...] = x + mix


def _even_mixer(x, j, layer_params, batch_major_in):
    rows = EVEN_TQ * BATCH
    x_spec = _batch_rows(EVEN_TQ, D_MODEL) if batch_major_in else _rows(rows, D_MODEL)
    scratch = [
        pltpu.VMEM((rows + CONV_HALO, D_CONV), f32),
        pltpu.VMEM((rows, SSM_COLS), f32),
        pltpu.VMEM((BATCH, SSM_COLS), f32),
    ]
    if batch_major_in:
        scratch.append(pltpu.VMEM((D_MODEL // LANES, rows, LANES), f32))
    return pl.pallas_call(
        _even_kernel,
        grid=(N_ROWS // rows,),
        in_specs=[x_spec] + [_layer_spec(a, j) for a in layer_params],
        out_specs=_rows(rows, D_MODEL),
        out_shape=jax.ShapeDtypeStruct((N_ROWS, D_MODEL), f32),
        scratch_shapes=scratch,
        compiler_params=_params(),
        name="even_mixer",
    )(x, *layer_params)


def _odd_kernel(x_ref, g_ref, win_ref, pw_ref, ps_ref, ng_ref, sw_ref, sb_ref, wout_ref,
                o_ref, z_ref, v_ref, m_ref):
    i = pl.program_id(0)
    rows = x_ref.shape[0]
    tq = rows // BATCH
    assert tq == CHUNK

    @pl.when(i == 0)
    def _():
        z_ref[0:POOL_HALO, :] = jnp.zeros((POOL_HALO, D_POOL), f32)

    x = x_ref[...]
    h = _rms(x, g_ref[...]).astype(bf16)
    proj = _dot(h, win_ref[...])
    z = proj[:, 0:D_POOL]
    uv = jax.nn.gelu(proj[:, D_POOL:])
    su = uv[:, 0:D_SGU]
    sv = uv[:, D_SGU:]

    z_ref[POOL_HALO:POOL_HALO + rows, :] = z
    count = (i * tq + 1 + lax.broadcasted_iota(jnp.int32, (rows, 1), 0) // BATCH).astype(f32)
    yc = []
    for gi, win in enumerate(POOL_WINDOWS):
        cols = slice(gi * POOL_GROUP, (gi + 1) * POOL_GROUP)
        e = z_ref[:, cols]
        span = 1
        while span < win:
            sh = span * BATCH
            e = e[sh:, :] + e[:-sh, :]
            span *= 2
        wsum = e[e.shape[0] - rows:, :]
        pooled = wsum / jnp.minimum(count, float(win)) - z[:, cols]
        yc.append(_dot(pooled.astype(bf16), pw_ref[gi]))
    yc = jnp.concatenate(yc, axis=1) * ps_ref[...]
    z_ref[0:POOL_HALO, :] = z_ref[rows:rows + POOL_HALO, :]

    v = _rms(sv, ng_ref[...])
    tri = lax.broadcasted_iota(jnp.int32, (CHUNK, CHUNK), 0) >= lax.broadcasted_iota(
        jnp.int32, (CHUNK, CHUNK), 1)
    for hd in range(SGU_HEADS):
        v_ref[hd] = v[:, hd * SGU_HEAD_DIM:(hd + 1) * SGU_HEAD_DIM]
    for hd in range(SGU_HEADS):
        w_s = jnp.where(tri, sw_ref[hd], 0.0).astype(bf16)
        v_b = jnp.concatenate(
            [v_ref[hd, pl.ds(b, CHUNK, stride=BATCH), :].astype(bf16) for b in range(BATCH)],
            axis=1)
        mixed_b = _dot(w_s, v_b) + sb_ref[:, hd:hd + 1]
        for b in range(BATCH):
            m_ref[hd, pl.ds(b, CHUNK, stride=BATCH), :] = mixed_b[:, b * SGU_HEAD_DIM:(b + 1) * SGU_HEAD_DIM]
    yd = su * jnp.concatenate([m_ref[hd] for hd in range(SGU_HEADS)], axis=1)

    mix = _dot(yc.astype(bf16), wout_ref[0:D_POOL, :]) + _dot(yd.astype(bf16), wout_ref[D_POOL:, :])
    o_ref[...] = x + mix


def _odd_mixer(x, j, layer_params):
    rows = ODD_TQ * BATCH
    return pl.pallas_call(
        _odd_kernel,
        grid=(N_ROWS // rows,),
        in_specs=[_rows(rows, D_MODEL)] + [_layer_spec(a, j) for a in layer_params],
        out_specs=_rows(rows, D_MODEL),
        out_shape=jax.ShapeDtypeStruct((N_ROWS, D_MODEL), f32),
        scratch_shapes=[
            pltpu.VMEM((rows + POOL_HALO, D_POOL), f32),
            pltpu.VMEM((SGU_HEADS, rows, SGU_HEAD_DIM), f32),
            pltpu.VMEM((SGU_HEADS, rows, SGU_HEAD_DIM), f32),
        ],
        compiler_params=_params(),
        name="odd_mixer",
    )(x, *layer_params)


def _ffn_chunks():
    out, c0 = [], 0
    while c0 < D_FF:
        cw = min(512, D_FF - c0)
        out.append((c0, cw))
        c0 += cw
    return out


def _ffn_kernel(x_ref, g_ref, wup_ref, cw_ref, cb_ref, wdn_ref, gf_ref, o_ref, up_ref, act_ref,
                yt_ref=None):
    i = pl.program_id(0)
    rows = x_ref.shape[0]
    tq = rows // BATCH

    @pl.when(i == 0)
    def _():
        up_ref[...] = jnp.zeros_like(up_ref)

    x = x_ref[...]
    h = _rms(x, g_ref[...]).astype(bf16)

    def conv(off, cw):
        cols = slice(off, off + cw)
        up = _dot(h, wup_ref[:, cols])
        ext = jnp.concatenate([up_ref[:, cols], up], axis=0)
        up_ref[:, cols] = up[rows - CONV_HALO:rows, :]
        y = ext[0:rows, :] * cw_ref[0:1, cols]
        for k in range(1, CONV_WIDTH):
            y = y + ext[k * BATCH:k * BATCH + rows, :] * cw_ref[k:k + 1, cols]
        return y + cb_ref[:, cols]

    for c0, cw in _ffn_chunks():
        gate = conv(c0, cw)
        val = conv(D_FF + c0, cw)
        act_ref[:, c0:c0 + cw] = (jax.nn.silu(gate) * val).astype(bf16)

    out = x + _dot(act_ref[...], wdn_ref[...])
    if yt_ref is None:
        o_ref[...] = out
    else:
        out = _rms(out, gf_ref[...])
        for c in range(D_MODEL // LANES):
            yt_ref[c] = out[:, c * LANES:(c + 1) * LANES]
        for b in range(BATCH):
            for c in range(D_MODEL // LANES):
                o_ref[b, :, c * LANES:(c + 1) * LANES] = yt_ref[c, pl.ds(b, tq, stride=BATCH), :]


def _conv_ffn(x, j, layer_params, g_final, final):
    rows = FFN_TQ * BATCH
    scratch = [pltpu.VMEM((CONV_HALO, 2 * D_FF), f32), pltpu.VMEM((rows, D_FF), bf16)]
    if final:
        scratch.append(pltpu.VMEM((D_MODEL // LANES, rows, LANES), f32))
        out_spec = _batch_rows(FFN_TQ, D_MODEL)
        out_shape = jax.ShapeDtypeStruct((BATCH, SEQ, D_MODEL), f32)
    else:
        out_spec = _rows(rows, D_MODEL)
        out_shape = jax.ShapeDtypeStruct((N_ROWS, D_MODEL), f32)
    return pl.pallas_call(
        _ffn_kernel,
        grid=(N_ROWS // rows,),
        in_specs=([_rows(rows, D_MODEL)] + [_layer_spec(a, j) for a in layer_params]
                  + [_layer_spec(g_final, 0)]),
        out_specs=out_spec,
        out_shape=out_shape,
        scratch_shapes=scratch,
        compiler_params=_params(),
        name="conv_ffn",
    )(x, *layer_params, g_final)


def kernel(x, norm_mix_g, even_w_in, even_conv_w, ssm_log_step, ssm_a_re, ssm_a_im, ssm_b_re, ssm_b_im, ssm_c_re, ssm_c_im, ssm_d, ssm_glu_w, ssm_glu_b, even_w_out, odd_w_in, pool_w, pool_scale, sgu_norm_g, sgu_w, sgu_b, odd_w_out, norm_ffn_g, ffn_w_up, ffn_conv_w, ffn_conv_b, ffn_w_down, norm_final_g):
    assert x.shape == (BATCH, SEQ, D_MODEL)
    n_even, n_odd = even_w_in.shape[0], odd_w_in.shape[0]
    rows3 = lambda v: v.reshape(v.shape[0], 1, -1)

    l_re, l_im, bb_re, bb_im = _s5_params(ssm_log_step, ssm_a_re, ssm_a_im, ssm_b_re, ssm_b_im)
    shape_gph = (n_even, N_SSM_GROUPS, SSM_STATE, SSM_GROUP)
    bmat = jnp.concatenate(
        [_block_diag_halves(jnp.swapaxes(bb_re.reshape(shape_gph), 2, 3)),
         _block_diag_halves(jnp.swapaxes(bb_im.reshape(shape_gph), 2, 3))],
        axis=3).astype(bf16)
    cmat = jnp.concatenate(
        [_block_diag_halves(jnp.swapaxes(ssm_c_re, 2, 3)),
         _block_diag_halves(-jnp.swapaxes(ssm_c_im, 2, 3))],
        axis=2).astype(bf16)
    even_params = (
        rows3(norm_mix_g[0::2]), even_w_in.astype(bf16), even_conv_w,
        l_re.reshape(n_even, 1, -1), l_im.reshape(n_even, 1, -1), bmat, cmat, rows3(ssm_d),
        ssm_glu_w.astype(bf16), rows3(ssm_glu_b), even_w_out.astype(bf16))

    odd_params = (
        rows3(norm_mix_g[1::2]), odd_w_in.astype(bf16), pool_w.astype(bf16),
        rows3(pool_scale), rows3(sgu_norm_g), sgu_w, jnp.swapaxes(sgu_b, 1, 2),
        odd_w_out.astype(bf16))

    ffn_params = (
        rows3(norm_ffn_g), ffn_w_up.astype(bf16), ffn_conv_w, rows3(ffn_conv_b),
        ffn_w_down.astype(bf16))
    g_final = norm_final_g.reshape(1, 1, D_MODEL)

    xt = x
    for i in range(DEPTH):
        j = i // 2
        if i % 2 == 0:
            xt = _even_mixer(xt, j, even_params, batch_major_in=(i == 0))
        else:
            xt = _odd_mixer(xt, j, odd_params)
        xt = _conv_ffn(xt, i, ffn_params, g_final, final=(i == DEPTH - 1))
    return xt
```

```python
import jax
import jax.numpy as jnp
from jax import lax
from jax.experimental import pallas as pl
from jax.experimental.pallas import tpu as pltpu

D_MODEL = 1024
BATCH = 8
SEQ = 4096
DEPTH = 4
D_CONV = D_MODEL // 2
CONV_WIDTH = 3
D_SSM = D_MODEL // 2
SSM_GROUP = 16
N_SSM_GROUPS = D_SSM // SSM_GROUP
SSM_STATE = 64
D_POOL = D_MODEL // 2
POOL_WINDOWS = (2, 4, 8, 16)
POOL_GROUP = D_POOL // len(POOL_WINDOWS)
D_SGU = D_MODEL // 2
SGU_HEADS = 4
SGU_HEAD_DIM = D_SGU // SGU_HEADS
CHUNK = 128
D_FF = ((8 * D_MODEL // 3 + 127) // 128) * 128
EPS = 1e-6

N_ROWS = SEQ * BATCH
LANES = 128
V7X_VMEM_BYTES = 64 * 1024 * 1024
VMEM_LIMIT_BYTES = V7X_VMEM_BYTES - 8 * 1024 * 1024

CONV_HALO = (CONV_WIDTH - 1) * BATCH
POOL_HALO = max(POOL_WINDOWS) * BATCH
SSM_HALF = D_SSM // 2
SSM_HALF_STATES = N_SSM_GROUPS // 2 * SSM_STATE
SSM_COLS = 4 * SSM_HALF_STATES
SCAN_COLS = 512

N_SCAN_BLOCKS = SSM_COLS // (2 * SCAN_COLS)
EVEN_TQ = 256
EVEN_TQ_BATCH_MAJOR = 128
EVEN_SUB_TQ = 64
FFN_TQ = 128
ODD_TQ = CHUNK

f32 = jnp.float32
bf16 = jnp.bfloat16


def _rms(x, g):
    return x * lax.rsqrt(jnp.mean(x * x, axis=-1, keepdims=True) + EPS) * g


def _dot(a, b):
    return jnp.dot(a, b, preferred_element_type=f32)


def _layer_spec(arr, j):
    nd = arr.ndim
    return pl.BlockSpec((None,) + arr.shape[1:], lambda i: (j,) + (0,) * (nd - 1),
                        pipeline_mode=pl.Buffered(1))


def _rows(r, d):
    return pl.BlockSpec((r, d), lambda i: (i, 0))


def _batch_rows(tq, d):
    return pl.BlockSpec((BATCH, tq, d), lambda i: (0, i, 0))


def _params():
    return pltpu.CompilerParams(dimension_semantics=("arbitrary",),
                                vmem_limit_bytes=VMEM_LIMIT_BYTES)


def _s5_param_kernel(ls_ref, are_ref, aim_ref, bre_ref, bim_ref,
                     lre_ref, lim_ref, obre_ref, obim_ref):
    step = jnp.exp(ls_ref[...])
    a_re = are_ref[...]
    a_im = aim_ref[...]
    mag = jnp.exp(a_re * step)
    l_re = mag * jnp.cos(a_im * step)
    l_im = mag * jnp.sin(a_im * step)
    lre_ref[...] = l_re
    lim_ref[...] = l_im
    n_re = l_re - 1.0
    den = a_re * a_re + a_im * a_im
    c_re = (n_re * a_re + l_im * a_im) / den
    c_im = (l_im * a_re - n_re * a_im) / den
    b_re = bre_ref[...]
    b_im = bim_ref[...]
    obre_ref[...] = c_re * b_re - c_im * b_im
    obim_ref[...] = c_re * b_im + c_im * b_re


def _s5_params(log_step, a_re, a_im, b_re, b_im):
    gp = log_step.shape[0] * N_SSM_GROUPS * SSM_STATE
    ls = jnp.repeat(log_step.reshape(-1), SSM_STATE).reshape(gp, 1)
    col = jax.ShapeDtypeStruct((gp, 1), f32)
    mat = jax.ShapeDtypeStruct((gp, SSM_GROUP), f32)
    return pl.pallas_call(
        _s5_param_kernel,
        out_shape=(col, col, mat, mat),
        name="s5_params",
    )(ls, a_re.reshape(gp, 1), a_im.reshape(gp, 1),
      b_re.reshape(gp, SSM_GROUP), b_im.reshape(gp, SSM_GROUP))


def _block_diag_halves(w):
    *lead, g, a, b = w.shape
    gh = g // 2
    w = w.reshape(*lead, 2, gh, a, 1, b)
    eye = jnp.eye(gh, dtype=bool).reshape(gh, 1, gh, 1)
    return jnp.where(eye, w, 0.0).reshape(*lead, 2, gh * a, gh * b)


def _even_kernel(x_ref, g_ref, win_ref, cw_ref, ar_ref, ai_ref, bm_ref, cm_ref,
                 dsk_ref, gw_ref, gb_ref, wout_ref, o_ref, cx_ref, bu_ref, st_ref,
                 xt_ref=None):
    i = pl.program_id(0)
    tile_rows = o_ref.shape[0]

    @pl.when(i == 0)
    def _():
        cx_ref[0:CONV_HALO, :] = jnp.zeros((CONV_HALO, D_CONV), f32)
        st_ref[...] = jnp.zeros_like(st_ref)

    if xt_ref is not None:
        for b in range(BATCH):
            for c in range(D_MODEL // LANES):
                xt_ref[c, pl.ds(b, tile_rows // BATCH, stride=BATCH), :] = (
                    x_ref[b, :, c * LANES:(c + 1) * LANES])

    rows = EVEN_SUB_TQ * BATCH
    tq = EVEN_SUB_TQ
    n_sub = tile_rows // rows
    assert D_CONV == D_SSM and N_SCAN_BLOCKS == 4

    def normed(sub):
        r_lo = sub * rows
        if xt_ref is None:
            x = x_ref[r_lo:r_lo + rows, :]
        else:
            x = jnp.concatenate(
                [xt_ref[c, r_lo:r_lo + rows, :] for c in range(D_MODEL // LANES)], axis=1)
        return x, _rms(x, g_ref[...]).astype(bf16)

    def in_slice(h, q):
        return _dot(h, win_ref[:, q * D_CONV:(q + 1) * D_CONV])

    x, h = normed(0)
    xa, ba, ca, u = [in_slice(h, q) for q in range(4)]
    ya = _even_conv(xa, ba, ca, cw_ref, cx_ref)
    ub = u.astype(bf16)
    for blk in range(N_SCAN_BLOCKS):
        _s5_expand(ub, blk, bm_ref, bu_ref)

    for sub in range(n_sub):
        last = sub + 1 == n_sub
        if not last:
            x_n, h_n = normed(sub + 1)
            parts_n = []
        ys = [None, None]
        for blk in range(N_SCAN_BLOCKS):
            _s5_scan(blk, tq, ar_ref, ai_ref, bu_ref, st_ref)
            if not last:
                parts_n.append(in_slice(h_n, blk))
            part = _s5_readout(blk, cm_ref, bu_ref)
            k = blk // (N_SCAN_BLOCKS // 2)
            ys[k] = part if ys[k] is None else ys[k] + part
        y = jax.nn.gelu(jnp.concatenate(ys, axis=1) + dsk_ref[...] * u)
        if not last:
            ya_n = _even_conv(parts_n[0], parts_n[1], parts_n[2], cw_ref, cx_ref)
            u_n = parts_n[3]
            ub_n = u_n.astype(bf16)
            for blk in range(N_SCAN_BLOCKS // 2):
                _s5_expand(ub_n, blk, bm_ref, bu_ref)
        gate = _dot(y.astype(bf16), gw_ref[...]) + gb_ref[...]
        if not last:
            for blk in range(N_SCAN_BLOCKS // 2, N_SCAN_BLOCKS):
                _s5_expand(ub_n, blk, bm_ref, bu_ref)
        yb = y * jax.nn.sigmoid(gate)
        mix = _dot(ya.astype(bf16), wout_ref[0:D_CONV, :]) + _dot(yb.astype(bf16), wout_ref[D_CONV:, :])
        o_ref[sub * rows:(sub + 1) * rows, :] = x + mix
        if not last:
            x, ya, u = x_n, ya_n, u_n


def _even_conv(xa, ba, ca, cw_ref, cx_ref):
    rows = xa.shape[0]
    cx_ref[CONV_HALO:CONV_HALO + rows, :] = ca * xa
    conv = cx_ref[0:rows, :] * cw_ref[0:1, :]
    for k in range(1, CONV_WIDTH):
        conv = conv + cx_ref[k * BATCH:k * BATCH + rows, :] * cw_ref[k:k + 1, :]
    ya = ba * conv
    cx_ref[0:CONV_HALO, :] = cx_ref[rows:rows + CONV_HALO, :]
    return ya


def _scan_block_cols(blk):
    k, cb = divmod(blk, SSM_HALF_STATES // SCAN_COLS)
    c_re = 2 * SSM_HALF_STATES * k + cb * SCAN_COLS
    w_re = cb * SCAN_COLS
    return (k, c_re, c_re + SSM_HALF_STATES, w_re, w_re + SSM_HALF_STATES,
            SSM_HALF_STATES * k + cb * SCAN_COLS)


def _s5_expand(ub, blk, bm_ref, bu_ref):
    k, c_re, c_im, w_re, w_im, _ = _scan_block_cols(blk)
    uk = ub[:, SSM_HALF * k:SSM_HALF * (k + 1)]
    bu_ref[:, c_re:c_re + SCAN_COLS] = _dot(uk, bm_ref[k, :, w_re:w_re + SCAN_COLS])
    bu_ref[:, c_im:c_im + SCAN_COLS] = _dot(uk, bm_ref[k, :, w_im:w_im + SCAN_COLS])


def _s5_scan(blk, tq, ar_ref, ai_ref, bu_ref, st_ref):
    _, c_re, c_im, _, _, a_col = _scan_block_cols(blk)
    ar = jnp.broadcast_to(ar_ref[:, a_col:a_col + SCAN_COLS], (BATCH, SCAN_COLS))
    ai = jnp.broadcast_to(ai_ref[:, a_col:a_col + SCAN_COLS], (BATCH, SCAN_COLS))
    s_re = st_ref[:, c_re:c_re + SCAN_COLS]
    s_im = st_ref[:, c_im:c_im + SCAN_COLS]
    for t in range(tq):
        r0 = t * BATCH
        n_re = ar * s_re - ai * s_im + bu_ref[r0:r0 + BATCH, c_re:c_re + SCAN_COLS]
        n_im = ar * s_im + ai * s_re + bu_ref[r0:r0 + BATCH, c_im:c_im + SCAN_COLS]
        bu_ref[r0:r0 + BATCH, c_re:c_re + SCAN_COLS] = n_re
        bu_ref[r0:r0 + BATCH, c_im:c_im + SCAN_COLS] = n_im
        s_re, s_im = n_re, n_im
    st_ref[:, c_re:c_re + SCAN_COLS] = s_re
    st_ref[:, c_im:c_im + SCAN_COLS] = s_im


def _s5_readout(blk, cm_ref, bu_ref):
    k, c_re, c_im, w_re, w_im, _ = _scan_block_cols(blk)
    return (_dot(bu_ref[:, c_re:c_re + SCAN_COLS].astype(bf16), cm_ref[k, w_re:w_re + SCAN_COLS, :])
            + _dot(bu_ref[:, c_im:c_im + SCAN_COLS].astype(bf16), cm_ref[k, w_im:w_im + SCAN_COLS, :]))


def _even_mixer(x, j, layer_params, batch_major_in):
    tq = EVEN_TQ_BATCH_MAJOR if batch_major_in else EVEN_TQ
    rows = tq * BATCH
    sub_rows = EVEN_SUB_TQ * BATCH
    x_spec = _batch_rows(tq, D_MODEL) if batch_major_in else _rows(rows, D_MODEL)
    scratch = [
        pltpu.VMEM((sub_rows + CONV_HALO, D_CONV), f32),
        pltpu.VMEM((sub_rows, SSM_COLS), f32),
        pltpu.VMEM((BATCH, SSM_COLS), f32),
    ]
    if batch_major_in:
        scratch.append(pltpu.VMEM((D_MODEL // LANES, rows, LANES), f32))
    return pl.pallas_call(
        _even_kernel,
        grid=(N_ROWS // rows,),
        in_specs=[x_spec] + [_layer_spec(a, j) for a in layer_params],
        out_specs=_rows(rows, D_MODEL),
        out_shape=jax.ShapeDtypeStruct((N_ROWS, D_MODEL), f32),
        scratch_shapes=scratch,
        compiler_params=_params(),
        name="even_mixer",
    )(x, *layer_params)


def _odd_kernel(x_ref, g_ref, win_ref, pw_ref, ps_ref, ng_ref, sw_ref, sb_ref, wout_ref,
                o_ref, z_ref, v_ref, m_ref):
    i = pl.program_id(0)
    rows = x_ref.shape[0]
    tq = rows // BATCH
    assert tq == CHUNK

    @pl.when(i == 0)
    def _():
        z_ref[0:POOL_HALO, :] = jnp.zeros((POOL_HALO, D_POOL), f32)

    x = x_ref[...]
    h = _rms(x, g_ref[...]).astype(bf16)
    proj = _dot(h, win_ref[...])
    z = proj[:, 0:D_POOL]
    uv = jax.nn.gelu(proj[:, D_POOL:])
    su = uv[:, 0:D_SGU]
    sv = uv[:, D_SGU:]

    z_ref[POOL_HALO:POOL_HALO + rows, :] = z
    count = (i * tq + 1 + lax.broadcasted_iota(jnp.int32, (rows, 1), 0) // BATCH).astype(f32)
    yc = []
    for gi, win in enumerate(POOL_WINDOWS):
        cols = slice(gi * POOL_GROUP, (gi + 1) * POOL_GROUP)
        e = z_ref[:, cols]
        span = 1
        while span < win:
            sh = span * BATCH
            e = e[sh:, :] + e[:-sh, :]
            span *= 2
        wsum = e[e.shape[0] - rows:, :]
        pooled = wsum / jnp.minimum(count, float(win)) - z[:, cols]
        yc.append(_dot(pooled.astype(bf16), pw_ref[gi]))
    yc = jnp.concatenate(yc, axis=1) * ps_ref[...]
    z_ref[0:POOL_HALO, :] = z_ref[rows:rows + POOL_HALO, :]

    v = _rms(sv, ng_ref[...])
    tri = lax.broadcasted_iota(jnp.int32, (CHUNK, CHUNK), 0) >= lax.broadcasted_iota(
        jnp.int32, (CHUNK, CHUNK), 1)
    for hd in range(SGU_HEADS):
        v_ref[hd] = v[:, hd * SGU_HEAD_DIM:(hd + 1) * SGU_HEAD_DIM]
    for hd in range(SGU_HEADS):
        w_s = jnp.where(tri, sw_ref[hd], 0.0).astype(bf16)
        v_b = jnp.concatenate(
            [v_ref[hd, pl.ds(b, CHUNK, stride=BATCH), :].astype(bf16) for b in range(BATCH)],
            axis=1)
        mixed_b = _dot(w_s, v_b) + sb_ref[:, hd:hd + 1]
        for b in range(BATCH):
            m_ref[hd, pl.ds(b, CHUNK, stride=BATCH), :] = mixed_b[:, b * SGU_HEAD_DIM:(b + 1) * SGU_HEAD_DIM]
    yd = su * jnp.concatenate([m_ref[hd] for hd in range(SGU_HEADS)], axis=1)

    mix = _dot(yc.astype(bf16), wout_ref[0:D_POOL, :]) + _dot(yd.astype(bf16), wout_ref[D_POOL:, :])
    o_ref[...] = x + mix


def _odd_mixer(x, j, layer_params):
    rows = ODD_TQ * BATCH
    return pl.pallas_call(
        _odd_kernel,
        grid=(N_ROWS // rows,),
        in_specs=[_rows(rows, D_MODEL)] + [_layer_spec(a, j) for a in layer_params],
        out_specs=_rows(rows, D_MODEL),
        out_shape=jax.ShapeDtypeStruct((N_ROWS, D_MODEL), f32),
        scratch_shapes=[
            pltpu.VMEM((rows + POOL_HALO, D_POOL), f32),
            pltpu.VMEM((SGU_HEADS, rows, SGU_HEAD_DIM), f32),
            pltpu.VMEM((SGU_HEADS, rows, SGU_HEAD_DIM), f32),
        ],
        compiler_params=_params(),
        name="odd_mixer",
    )(x, *layer_params)


def _ffn_chunks():
    out, c0 = [], 0
    while c0 < D_FF:
        cw = min(512, D_FF - c0)
        out.append((c0, cw))
        c0 += cw
    return out


def _ffn_kernel(x_ref, g_ref, wup_ref, cw_ref, cb_ref, wdn_ref, gf_ref, o_ref, up_ref, act_ref,
                yt_ref=None):
    i = pl.program_id(0)
    rows = x_ref.shape[0]
    tq = rows // BATCH

    @pl.when(i == 0)
    def _():
        up_ref[...] = jnp.zeros_like(up_ref)

    x = x_ref[...]
    h = _rms(x, g_ref[...]).astype(bf16)

    def conv(off, cw):
        cols = slice(off, off + cw)
        up = _dot(h, wup_ref[:, cols])
        ext = jnp.concatenate([up_ref[:, cols], up], axis=0)
        up_ref[:, cols] = up[rows - CONV_HALO:rows, :]
        y = ext[0:rows, :] * cw_ref[0:1, cols]
        for k in range(1, CONV_WIDTH):
            y = y + ext[k * BATCH:k * BATCH + rows, :] * cw_ref[k:k + 1, cols]
        return y + cb_ref[:, cols]

    for c0, cw in _ffn_chunks():
        gate = conv(c0, cw)
        val = conv(D_FF + c0, cw)
        act_ref[:, c0:c0 + cw] = (jax.nn.silu(gate) * val).astype(bf16)

    out = x + _dot(act_ref[...], wdn_ref[...])
    if yt_ref is None:
        o_ref[...] = out
    else:
        out = _rms(out, gf_ref[...])
        for c in range(D_MODEL // LANES):
            yt_ref[c] = out[:, c * LANES:(c + 1) * LANES]
        for b in range(BATCH):
            for c in range(D_MODEL // LANES):
                o_ref[b, :, c * LANES:(c + 1) * LANES] = yt_ref[c, pl.ds(b, tq, stride=BATCH), :]


def _conv_ffn(x, j, layer_params, g_final, final):
    rows = FFN_TQ * BATCH
    scratch = [pltpu.VMEM((CONV_HALO, 2 * D_FF), f32), pltpu.VMEM((rows, D_FF), bf16)]
    if final:
        scratch.append(pltpu.VMEM((D_MODEL // LANES, rows, LANES), f32))
        out_spec = _batch_rows(FFN_TQ, D_MODEL)
        out_shape = jax.ShapeDtypeStruct((BATCH, SEQ, D_MODEL), f32)
    else:
        out_spec = _rows(rows, D_MODEL)
        out_shape = jax.ShapeDtypeStruct((N_ROWS, D_MODEL), f32)
    return pl.pallas_call(
        _ffn_kernel,
        grid=(N_ROWS // rows,),
        in_specs=([_rows(rows, D_MODEL)] + [_layer_spec(a, j) for a in layer_params]
                  + [_layer_spec(g_final, 0)]),
        out_specs=out_spec,
        out_shape=out_shape,
        scratch_shapes=scratch,
        compiler_params=_params(),
        name="conv_ffn",
    )(x, *layer_params, g_final)


def kernel(x, norm_mix_g, even_w_in, even_conv_w, ssm_log_step, ssm_a_re, ssm_a_im, ssm_b_re, ssm_b_im, ssm_c_re, ssm_c_im, ssm_d, ssm_glu_w, ssm_glu_b, even_w_out, odd_w_in, pool_w, pool_scale, sgu_norm_g, sgu_w, sgu_b, odd_w_out, norm_ffn_g, ffn_w_up, ffn_conv_w, ffn_conv_b, ffn_w_down, norm_final_g):
    assert x.shape == (BATCH, SEQ, D_MODEL)
    n_even, n_odd = even_w_in.shape[0], odd_w_in.shape[0]
    rows3 = lambda v: v.reshape(v.shape[0], 1, -1)

    l_re, l_im, bb_re, bb_im = _s5_params(ssm_log_step, ssm_a_re, ssm_a_im, ssm_b_re, ssm_b_im)
    shape_gph = (n_even, N_SSM_GROUPS, SSM_STATE, SSM_GROUP)
    bmat = jnp.concatenate(
        [_block_diag_halves(jnp.swapaxes(bb_re.reshape(shape_gph), 2, 3)),
         _block_diag_halves(jnp.swapaxes(bb_im.reshape(shape_gph), 2, 3))],
        axis=3).astype(bf16)
    cmat = jnp.concatenate(
        [_block_diag_halves(jnp.swapaxes(ssm_c_re, 2, 3)),
         _block_diag_halves(-jnp.swapaxes(ssm_c_im, 2, 3))],
        axis=2).astype(bf16)
    even_params = (
        rows3(norm_mix_g[0::2]), even_w_in.astype(bf16), even_conv_w,
        l_re.reshape(n_even, 1, -1), l_im.reshape(n_even, 1, -1), bmat, cmat, rows3(ssm_d),
        ssm_glu_w.astype(bf16), rows3(ssm_glu_b), even_w_out.astype(bf16))

    odd_params = (
        rows3(norm_mix_g[1::2]), odd_w_in.astype(bf16), pool_w.astype(bf16),
        rows3(pool_scale), rows3(sgu_norm_g), sgu_w, jnp.swapaxes(sgu_b, 1, 2),
        odd_w_out.astype(bf16))

    ffn_params = (
        rows3(norm_ffn_g), ffn_w_up.astype(bf16), ffn_conv_w, rows3(ffn_conv_b),
        ffn_w_down.astype(bf16))
    g_final = norm_final_g.reshape(1, 1, D_MODEL)

    xt = x
    for i in range(DEPTH):
        j = i // 2
        if i % 2 == 0:
            xt = _even_mixer(xt, j, even_params, batch_major_in=(i == 0))
        else:
            xt = _odd_mixer(xt, j, odd_params)
        xt = _conv_ffn(xt, i, ffn_params, g_final, final=(i == DEPTH - 1))
    return xt
```

```python
import jax
import jax.numpy as jnp
from jax import lax
from jax.experimental import pallas as pl
from jax.experimental.pallas import tpu as pltpu

D_MODEL = 1024
BATCH = 8
SEQ = 4096
DEPTH = 4
D_CONV = D_MODEL // 2
CONV_WIDTH = 3
D_SSM = D_MODEL // 2
SSM_GROUP = 16
N_SSM_GROUPS = D_SSM // SSM_GROUP
SSM_STATE = 64
D_POOL = D_MODEL // 2
POOL_WINDOWS = (2, 4, 8, 16)
POOL_GROUP = D_POOL // len(POOL_WINDOWS)
D_SGU = D_MODEL // 2
SGU_HEADS = 4
SGU_HEAD_DIM = D_SGU // SGU_HEADS
CHUNK = 128
D_FF = ((8 * D_MODEL // 3 + 127) // 128) * 128
EPS = 1e-6

N_ROWS = SEQ * BATCH
LANES = 128
V7X_VMEM_BYTES = 64 * 1024 * 1024
VMEM_LIMIT_BYTES = V7X_VMEM_BYTES - 8 * 1024 * 1024

CONV_HALO = (CONV_WIDTH - 1) * BATCH
POOL_HALO = max(POOL_WINDOWS) * BATCH
SSM_Q = 4
SLAB_GROUPS = LANES // SSM_GROUP
N_SLABS = N_SSM_GROUPS // SLAB_GROUPS
SLAB_STATES = SLAB_GROUPS * SSM_STATE
SSM_COLS = 2 * N_SLABS * SLAB_STATES

EVEN_TQ = 64
EVEN_TQ_BATCH_MAJOR = 64
EVEN_SUB_TQ = 64
FFN_TQ = 128
ODD_TQ = CHUNK

f32 = jnp.float32
bf16 = jnp.bfloat16


def _rms(x, g):
    return x * lax.rsqrt(jnp.mean(x * x, axis=-1, keepdims=True) + EPS) * g


def _dot(a, b):
    return jnp.dot(a, b, preferred_element_type=f32)


def _layer_spec(arr, j):
    nd = arr.ndim
    return pl.BlockSpec((None,) + arr.shape[1:], lambda i: (j,) + (0,) * (nd - 1),
                        pipeline_mode=pl.Buffered(1))


def _rows(r, d):
    return pl.BlockSpec((r, d), lambda i: (i, 0))


def _batch_rows(tq, d):
    return pl.BlockSpec((BATCH, tq, d), lambda i: (0, i, 0))


def _params():
    return pltpu.CompilerParams(dimension_semantics=("arbitrary",),
                                vmem_limit_bytes=VMEM_LIMIT_BYTES)


def _s5_param_kernel(ls_ref, are_ref, aim_ref, btre_ref, btim_ref, cre_ref, cim_ref,
                     ere_ref, eim_ref, qre_ref, qim_ref, t_ref, lre_ref, lim_ref):
    step = jnp.exp(ls_ref[0])
    a_re = are_ref[0]
    a_im = aim_ref[0]

    def power(m):
        mag = jnp.exp((m * a_re) * step)
        ang = (m * a_im) * step
        return mag * jnp.cos(ang), mag * jnp.sin(ang)

    l_re, l_im = power(1)
    n_re = l_re - 1.0
    den = a_re * a_re + a_im * a_im
    k_re = (n_re * a_re + l_im * a_im) / den
    k_im = (l_im * a_re - n_re * a_im) / den
    bt_re = btre_ref[0]
    bt_im = btim_ref[0]
    bb_re = k_re * bt_re - k_im * bt_im
    bb_im = k_re * bt_im + k_im * bt_re
    c_re = cre_ref[0]
    c_im = cim_ref[0]

    def dot_nt(a, b):
        return lax.dot_general(a, b, (((1,), (1,)), ((), ())),
                               precision=lax.Precision.HIGHEST, preferred_element_type=f32)

    for m in range(SSM_Q + 1):
        p_re, p_im = power(m)
        cp_re = c_re * p_re - c_im * p_im
        cp_im = c_re * p_im + c_im * p_re
        if m < SSM_Q:
            i = SSM_Q - 1 - m
            ere_ref[0, i] = p_re * bb_re - p_im * bb_im
            eim_ref[0, i] = p_re * bb_im + p_im * bb_re
            t_ref[0, m] = dot_nt(cp_re, bb_re) - dot_nt(cp_im, bb_im)
        if m >= 1:
            qre_ref[0, m - 1] = cp_re
            qim_ref[0, m - 1] = cp_im
    lq_re, lq_im = power(SSM_Q)
    lre_ref[0] = lq_re
    lim_ref[0] = lq_im


def _s5_params(log_step, a_re, a_im, b_re, b_im, c_re, c_im):
    n = log_step.shape[0] * N_SSM_GROUPS
    hg, p = SSM_GROUP, SSM_STATE

    def spec(*dims):
        return pl.BlockSpec((1,) + dims, lambda i: (i,) + (0,) * len(dims))

    shape = lambda *dims: jax.ShapeDtypeStruct((n,) + dims, f32)
    return pl.pallas_call(
        _s5_param_kernel,
        grid=(n,),
        in_specs=[spec(1, 1), spec(1, p), spec(1, p), spec(hg, p), spec(hg, p), spec(hg, p), spec(hg, p)],
        out_specs=(spec(SSM_Q, hg, p), spec(SSM_Q, hg, p), spec(SSM_Q, hg, p), spec(SSM_Q, hg, p),
                   spec(SSM_Q, hg, hg), spec(1, p), spec(1, p)),
        out_shape=(shape(SSM_Q, hg, p), shape(SSM_Q, hg, p), shape(SSM_Q, hg, p), shape(SSM_Q, hg, p),
                   shape(SSM_Q, hg, hg), shape(1, p), shape(1, p)),
        compiler_params=_params(),
        name="s5_params",
    )(log_step.reshape(n, 1, 1), a_re.reshape(n, 1, p), a_im.reshape(n, 1, p),
      jnp.swapaxes(b_re, 2, 3).reshape(n, hg, p), jnp.swapaxes(b_im, 2, 3).reshape(n, hg, p),
      c_re.reshape(n, hg, p), c_im.reshape(n, hg, p))


def _slab_block_diag(w, n_layers):
    _, a, r, c = w.shape
    gl = SLAB_GROUPS
    w = w.reshape(n_layers, N_SLABS, gl, a, r, 1, c)
    w = jnp.transpose(w, (0, 1, 3, 2, 4, 5, 6))
    eye = jnp.eye(gl, dtype=bool).reshape(gl, 1, gl, 1)
    return jnp.where(eye, w, 0.0).reshape(n_layers, N_SLABS, a * gl * r, gl * c)


def _s5_matrices(log_step, a_re, a_im, b_re, b_im, c_re, c_im):
    nl = log_step.shape[0]
    e_re, e_im, q_re, q_im, t, l_re, l_im = _s5_params(log_step, a_re, a_im, b_re, b_im, c_re, c_im)
    emat = jnp.concatenate([_slab_block_diag(e_re, nl), _slab_block_diag(e_im, nl)], axis=-1)
    def readout(q):
        blk = _slab_block_diag(jnp.swapaxes(q, 2, 3), nl)
        gl, p, hg = SLAB_GROUPS, SSM_STATE, SSM_GROUP
        blk = blk.reshape(nl, N_SLABS, SSM_Q, gl * p, gl * hg)
        return jnp.swapaxes(blk, 2, 3).reshape(nl, N_SLABS, gl * p, SSM_Q * gl * hg)
    qmat = jnp.concatenate([readout(q_re), -readout(q_im)], axis=-2)
    t_hi = jnp.swapaxes(t, 2, 3)
    zero = jnp.zeros_like(t_hi[:, 0])
    per_i = [jnp.concatenate([t_hi[:, j - i] if j >= i else zero for j in range(SSM_Q)], axis=-1)
             for i in range(SSM_Q)]
    tij = jnp.stack(per_i, axis=1)
    blk = _slab_block_diag(tij, nl)
    gl, hg = SLAB_GROUPS, SSM_GROUP
    blk = blk.reshape(nl, N_SLABS, SSM_Q * gl * hg, gl, SSM_Q, hg)
    tmat = jnp.swapaxes(blk, 3, 4).reshape(nl, N_SLABS, SSM_Q * gl * hg, SSM_Q * gl * hg)
    lam = lambda v: v.reshape(nl, 1, N_SSM_GROUPS * SSM_STATE)
    return emat.astype(bf16), qmat.astype(bf16), tmat.astype(bf16), lam(l_re), lam(l_im)


def _even_kernel(x_ref, g_ref, win_ref, cw_ref, lr_ref, li_ref, em_ref, qm_ref, tm_ref,
                 dsk_ref, gw_ref, gb_ref, wout_ref, o_ref, cx_ref, bu_ref, st_ref,
                 xt_ref=None):
    i = pl.program_id(0)
    tile_rows = o_ref.shape[0]

    @pl.when(i == 0)
    def _():
        cx_ref[0:CONV_HALO, :] = jnp.zeros((CONV_HALO, D_CONV), f32)
        st_ref[...] = jnp.zeros_like(st_ref)

    if xt_ref is not None:
        for b in range(BATCH):
            for c in range(D_MODEL // LANES):
                xt_ref[c, pl.ds(b, tile_rows // BATCH, stride=BATCH), :] = (
                    x_ref[b, :, c * LANES:(c + 1) * LANES])

    rows = EVEN_SUB_TQ * BATCH
    tq = EVEN_SUB_TQ
    n_sub = tile_rows // rows
    assert D_CONV == D_SSM and N_SLABS == 4

    def normed(sub):
        r_lo = sub * rows
        if xt_ref is None:
            x = x_ref[r_lo:r_lo + rows, :]
        else:
            x = jnp.concatenate(
                [xt_ref[c, r_lo:r_lo + rows, :] for c in range(D_MODEL // LANES)], axis=1)
        return x, _rms(x, g_ref[...]).astype(bf16)

    def in_slice(h, q):
        return _dot(h, win_ref[:, q * D_CONV:(q + 1) * D_CONV])

    x, h = normed(0)
    xa, ba, ca, u = [in_slice(h, q) for q in range(4)]
    ya = _even_conv(xa, ba, ca, cw_ref, cx_ref)
    u4 = _fold_time(u)
    for s in range(N_SLABS):
        _s5_expand(u4[s], s, em_ref, bu_ref)

    for sub in range(n_sub):
        last = sub + 1 == n_sub
        if not last:
            x_n, h_n = normed(sub + 1)
            parts_n = []
        y4 = []
        for s in range(N_SLABS):
            _s5_scan(s, tq // SSM_Q, lr_ref, li_ref, bu_ref, st_ref)
            if not last:
                parts_n.append(in_slice(h_n, s))
            y4.append(_s5_readout(s, u4[s], qm_ref, tm_ref, bu_ref))
        y = jax.nn.gelu(_unfold_time(y4) + dsk_ref[...] * u)
        if not last:
            ya_n = _even_conv(parts_n[0], parts_n[1], parts_n[2], cw_ref, cx_ref)
            u_n = parts_n[3]
            u4_n = _fold_time(u_n)
            for s in range(N_SLABS // 2):
                _s5_expand(u4_n[s], s, em_ref, bu_ref)
        gate = _dot(y.astype(bf16), gw_ref[...]) + gb_ref[...]
        if not last:
            for s in range(N_SLABS // 2, N_SLABS):
                _s5_expand(u4_n[s], s, em_ref, bu_ref)
        yb = y * jax.nn.sigmoid(gate)
        mix = _dot(ya.astype(bf16), wout_ref[0:D_CONV, :]) + _dot(yb.astype(bf16), wout_ref[D_CONV:, :])
        o_ref[sub * rows:(sub + 1) * rows, :] = x + mix
        if not last:
            x, ya, u, u4 = x_n, ya_n, u_n, u4_n


def _even_conv(xa, ba, ca, cw_ref, cx_ref):
    rows = xa.shape[0]
    cx_ref[CONV_HALO:CONV_HALO + rows, :] = ca * xa
    conv = cx_ref[0:rows, :] * cw_ref[0:1, :]
    for k in range(1, CONV_WIDTH):
        conv = conv + cx_ref[k * BATCH:k * BATCH + rows, :] * cw_ref[k:k + 1, :]
    ya = ba * conv
    cx_ref[0:CONV_HALO, :] = cx_ref[rows:rows + CONV_HALO, :]
    return ya


def _fold_time(u):
    n_chunks = u.shape[0] // (SSM_Q * BATCH)
    at_offset = [
        jnp.concatenate([u[(SSM_Q * k + i) * BATCH:(SSM_Q * k + i + 1) * BATCH, :]
                         for k in range(n_chunks)], axis=0)
        for i in range(SSM_Q)]
    return [jnp.concatenate([a[:, s * LANES:(s + 1) * LANES] for a in at_offset], axis=1).astype(bf16)
            for s in range(N_SLABS)]


def _unfold_time(y4):
    n_chunks = y4[0].shape[0] // BATCH
    at_offset = [jnp.concatenate([y[:, j * LANES:(j + 1) * LANES] for y in y4], axis=1)
                 for j in range(SSM_Q)]
    return jnp.concatenate([at_offset[j][k * BATCH:(k + 1) * BATCH, :]
                            for k in range(n_chunks) for j in range(SSM_Q)], axis=0)


def _s5_expand(u4s, s, em_ref, bu_ref):
    c0 = 2 * SLAB_STATES * s
    bu_ref[:, c0:c0 + 2 * SLAB_STATES] = _dot(u4s, em_ref[s])


def _s5_scan(s, n_chunks, lr_ref, li_ref, bu_ref, st_ref):
    c_re = 2 * SLAB_STATES * s
    c_im = c_re + SLAB_STATES
    re = slice(c_re, c_re + SLAB_STATES)
    im = slice(c_im, c_im + SLAB_STATES)
    lam = slice(SLAB_STATES * s, SLAB_STATES * (s + 1))
    ar = jnp.broadcast_to(lr_ref[:, lam], (BATCH, SLAB_STATES))
    ai = jnp.broadcast_to(li_ref[:, lam], (BATCH, SLAB_STATES))
    s_re = st_ref[:, re]
    s_im = st_ref[:, im]
    for k in range(n_chunks):
        r = slice(k * BATCH, (k + 1) * BATCH)
        b_re = bu_ref[r, re]
        b_im = bu_ref[r, im]
        bu_ref[r, re] = s_re
        bu_ref[r, im] = s_im
        s_re, s_im = ar * s_re - ai * s_im + b_re, ar * s_im + ai * s_re + b_im
    st_ref[:, re] = s_re
    st_ref[:, im] = s_im


def _s5_readout(s, u4s, qm_ref, tm_ref, bu_ref):
    c_re = 2 * SLAB_STATES * s
    c_im = c_re + SLAB_STATES
    return (_dot(bu_ref[:, c_re:c_re + SLAB_STATES].astype(bf16), qm_ref[s, 0:SLAB_STATES, :])
            + _dot(bu_ref[:, c_im:c_im + SLAB_STATES].astype(bf16), qm_ref[s, SLAB_STATES:, :])
            + _dot(u4s, tm_ref[s]))


def _even_mixer(x, j, layer_params, batch_major_in):
    tq = EVEN_TQ_BATCH_MAJOR if batch_major_in else EVEN_TQ
    rows = tq * BATCH
    sub_rows = EVEN_SUB_TQ * BATCH
    x_spec = _batch_rows(tq, D_MODEL) if batch_major_in else _rows(rows, D_MODEL)
    scratch = [
        pltpu.VMEM((sub_rows + CONV_HALO, D_CONV), f32),
        pltpu.VMEM((sub_rows // SSM_Q, SSM_COLS), f32),
        pltpu.VMEM((BATCH, SSM_COLS), f32),
    ]
    if batch_major_in:
        scratch.append(pltpu.VMEM((D_MODEL // LANES, rows, LANES), f32))
    return pl.pallas_call(
        _even_kernel,
        grid=(N_ROWS // rows,),
        in_specs=[x_spec] + [_layer_spec(a, j) for a in layer_params],
        out_specs=_rows(rows, D_MODEL),
        out_shape=jax.ShapeDtypeStruct((N_ROWS, D_MODEL), f32),
        scratch_shapes=scratch,
        compiler_params=_params(),
        name="even_mixer",
    )(x, *layer_params)


def _odd_kernel(x_ref, g_ref, win_ref, pw_ref, ps_ref, ng_ref, sw_ref, sb_ref, wout_ref,
                o_ref, z_ref, v_ref, m_ref):
    i = pl.program_id(0)
    rows = x_ref.shape[0]
    tq = rows // BATCH
    assert tq == CHUNK

    @pl.when(i == 0)
    def _():
        z_ref[0:POOL_HALO, :] = jnp.zeros((POOL_HALO, D_POOL), f32)

    x = x_ref[...]
    h = _rms(x, g_ref[...]).astype(bf16)
    proj = _dot(h, win_ref[...])
    z = proj[:, 0:D_POOL]
    uv = jax.nn.gelu(proj[:, D_POOL:])
    su = uv[:, 0:D_SGU]
    sv = uv[:, D_SGU:]

    z_ref[POOL_HALO:POOL_HALO + rows, :] = z
    count = (i * tq + 1 + lax.broadcasted_iota(jnp.int32, (rows, 1), 0) // BATCH).astype(f32)
    yc = []
    for gi, win in enumerate(POOL_WINDOWS):
        cols = slice(gi * POOL_GROUP, (gi + 1) * POOL_GROUP)
        e = z_ref[:, cols]
        span = 1
        while span < win:
            sh = span * BATCH
            e = e[sh:, :] + e[:-sh, :]
            span *= 2
        wsum = e[e.shape[0] - rows:, :]
        pooled = wsum / jnp.minimum(count, float(win)) - z[:, cols]
        yc.append(_dot(pooled.astype(bf16), pw_ref[gi]))
    yc = jnp.concatenate(yc, axis=1) * ps_ref[...]
    z_ref[0:POOL_HALO, :] = z_ref[rows:rows + POOL_HALO, :]

    v = _rms(sv, ng_ref[...])
    tri = lax.broadcasted_iota(jnp.int32, (CHUNK, CHUNK), 0) >= lax.broadcasted_iota(
        jnp.int32, (CHUNK, CHUNK), 1)
    for hd in range(SGU_HEADS):
        v_ref[hd] = v[:, hd * SGU_HEAD_DIM:(hd + 1) * SGU_HEAD_DIM]
    for hd in range(SGU_HEADS):
        w_s = jnp.where(tri, sw_ref[hd], 0.0).astype(bf16)
        v_b = jnp.concatenate(
            [v_ref[hd, pl.ds(b, CHUNK, stride=BATCH), :].astype(bf16) for b in range(BATCH)],
            axis=1)
        mixed_b = _dot(w_s, v_b) + sb_ref[:, hd:hd + 1]
        for b in range(BATCH):
            m_ref[hd, pl.ds(b, CHUNK, stride=BATCH), :] = mixed_b[:, b * SGU_HEAD_DIM:(b + 1) * SGU_HEAD_DIM]
    yd = su * jnp.concatenate([m_ref[hd] for hd in range(SGU_HEADS)], axis=1)

    mix = _dot(yc.astype(bf16), wout_ref[0:D_POOL, :]) + _dot(yd.astype(bf16), wout_ref[D_POOL:, :])
    o_ref[...] = x + mix


def _odd_mixer(x, j, layer_params):
    rows = ODD_TQ * BATCH
    return pl.pallas_call(
        _odd_kernel,
        grid=(N_ROWS // rows,),
        in_specs=[_rows(rows, D_MODEL)] + [_layer_spec(a, j) for a in layer_params],
        out_specs=_rows(rows, D_MODEL),
        out_shape=jax.ShapeDtypeStruct((N_ROWS, D_MODEL), f32),
        scratch_shapes=[
            pltpu.VMEM((rows + POOL_HALO, D_POOL), f32),
            pltpu.VMEM((SGU_HEADS, rows, SGU_HEAD_DIM), f32),
            pltpu.VMEM((SGU_HEADS, rows, SGU_HEAD_DIM), f32),
        ],
        compiler_params=_params(),
        name="odd_mixer",
    )(x, *layer_params)


def _ffn_chunks():
    out, c0 = [], 0
    while c0 < D_FF:
        cw = min(512, D_FF - c0)
        out.append((c0, cw))
        c0 += cw
    return out


def _ffn_kernel(x_ref, g_ref, wup_ref, cw_ref, cb_ref, wdn_ref, gf_ref, o_ref, up_ref, act_ref,
                yt_ref=None):
    i = pl.program_id(0)
    rows = x_ref.shape[0]
    tq = rows // BATCH

    @pl.when(i == 0)
    def _():
        up_ref[...] = jnp.zeros_like(up_ref)

    x = x_ref[...]
    h = _rms(x, g_ref[...]).astype(bf16)

    def conv(off, cw):
        cols = slice(off, off + cw)
        up = _dot(h, wup_ref[:, cols])
        ext = jnp.concatenate([up_ref[:, cols], up], axis=0)
        up_ref[:, cols] = up[rows - CONV_HALO:rows, :]
        y = ext[0:rows, :] * cw_ref[0:1, cols]
        for k in range(1, CONV_WIDTH):
            y = y + ext[k * BATCH:k * BATCH + rows, :] * cw_ref[k:k + 1, cols]
        return y + cb_ref[:, cols]

    for c0, cw in _ffn_chunks():
        gate = conv(c0, cw)
        val = conv(D_FF + c0, cw)
        act_ref[:, c0:c0 + cw] = (jax.nn.silu(gate) * val).astype(bf16)

    out = x + _dot(act_ref[...], wdn_ref[...])
    if yt_ref is None:
        o_ref[...] = out
    else:
        out = _rms(out, gf_ref[...])
        for c in range(D_MODEL // LANES):
            yt_ref[c] = out[:, c * LANES:(c + 1) * LANES]
        for b in range(BATCH):
            for c in range(D_MODEL // LANES):
                o_ref[b, :, c * LANES:(c + 1) * LANES] = yt_ref[c, pl.ds(b, tq, stride=BATCH), :]


def _conv_ffn(x, j, layer_params, g_final, final):
    rows = FFN_TQ * BATCH
    scratch = [pltpu.VMEM((CONV_HALO, 2 * D_FF), f32), pltpu.VMEM((rows, D_FF), bf16)]
    if final:
        scratch.append(pltpu.VMEM((D_MODEL // LANES, rows, LANES), f32))
        out_spec = _batch_rows(FFN_TQ, D_MODEL)
        out_shape = jax.ShapeDtypeStruct((BATCH, SEQ, D_MODEL), f32)
    else:
        out_spec = _rows(rows, D_MODEL)
        out_shape = jax.ShapeDtypeStruct((N_ROWS, D_MODEL), f32)
    return pl.pallas_call(
        _ffn_kernel,
        grid=(N_ROWS // rows,),
        in_specs=([_rows(rows, D_MODEL)] + [_layer_spec(a, j) for a in layer_params]
                  + [_layer_spec(g_final, 0)]),
        out_specs=out_spec,
        out_shape=out_shape,
        scratch_shapes=scratch,
        compiler_params=_params(),
        name="conv_ffn",
    )(x, *layer_params, g_final)


def kernel(x, norm_mix_g, even_w_in, even_conv_w, ssm_log_step, ssm_a_re, ssm_a_im, ssm_b_re, ssm_b_im, ssm_c_re, ssm_c_im, ssm_d, ssm_glu_w, ssm_glu_b, even_w_out, odd_w_in, pool_w, pool_scale, sgu_norm_g, sgu_w, sgu_b, odd_w_out, norm_ffn_g, ffn_w_up, ffn_conv_w, ffn_conv_b, ffn_w_down, norm_final_g):
    assert x.shape == (BATCH, SEQ, D_MODEL)
    n_even, n_odd = even_w_in.shape[0], odd_w_in.shape[0]
    rows3 = lambda v: v.reshape(v.shape[0], 1, -1)

    emat, qmat, tmat, l_re, l_im = _s5_matrices(
        ssm_log_step, ssm_a_re, ssm_a_im, ssm_b_re, ssm_b_im, ssm_c_re, ssm_c_im)
    even_params = (
        rows3(norm_mix_g[0::2]), even_w_in.astype(bf16), even_conv_w,
        l_re, l_im, emat, qmat, tmat, rows3(ssm_d),
        ssm_glu_w.astype(bf16), rows3(ssm_glu_b), even_w_out.astype(bf16))

    odd_params = (
        rows3(norm_mix_g[1::2]), odd_w_in.astype(bf16), pool_w.astype(bf16),
        rows3(pool_scale), rows3(sgu_norm_g), sgu_w, jnp.swapaxes(sgu_b, 1, 2),
        odd_w_out.astype(bf16))

    ffn_params = (
        rows3(norm_ffn_g), ffn_w_up.astype(bf16), ffn_conv_w, rows3(ffn_conv_b),
        ffn_w_down.astype(bf16))
    g_final = norm_final_g.reshape(1, 1, D_MODEL)

    xt = x
    for i in range(DEPTH):
        j = i // 2
        if i % 2 == 0:
            xt = _even_mixer(xt, j, even_params, batch_major_in=(i == 0))
        else:
            xt = _odd_mixer(xt, j, odd_params)
        xt = _conv_ffn(xt, i, ffn_params, g_final, final=(i == DEPTH - 1))
    return xt
```

```python
import jax
import jax.numpy as jnp
from jax import lax
from jax.experimental import pallas as pl
from jax.experimental.pallas import tpu as pltpu

D_MODEL = 1024
BATCH = 8
SEQ = 4096
DEPTH = 4
D_CONV = D_MODEL // 2
CONV_WIDTH = 3
D_SSM = D_MODEL // 2
SSM_GROUP = 16
N_SSM_GROUPS = D_SSM // SSM_GROUP
SSM_STATE = 64
D_POOL = D_MODEL // 2
POOL_WINDOWS = (2, 4, 8, 16)
POOL_GROUP = D_POOL // len(POOL_WINDOWS)
D_SGU = D_MODEL // 2
SGU_HEADS = 4
SGU_HEAD_DIM = D_SGU // SGU_HEADS
CHUNK = 128
D_FF = ((8 * D_MODEL // 3 + 127) // 128) * 128
EPS = 1e-6

N_ROWS = SEQ * BATCH
LANES = 128
V7X_VMEM_BYTES = 64 * 1024 * 1024
VMEM_LIMIT_BYTES = V7X_VMEM_BYTES - 8 * 1024 * 1024

CONV_HALO = (CONV_WIDTH - 1) * BATCH
POOL_HALO = max(POOL_WINDOWS) * BATCH
SSM_Q = 4
SLAB_GROUPS = LANES // SSM_GROUP
N_SLABS = N_SSM_GROUPS // SLAB_GROUPS
SLAB_STATES = SLAB_GROUPS * SSM_STATE
SSM_COLS = 2 * N_SLABS * SLAB_STATES

EVEN_TQ = 64
EVEN_TQ_BATCH_MAJOR = 64
EVEN_SUB_TQ = 64
FFN_TQ = 128
ODD_TQ = CHUNK

f32 = jnp.float32
bf16 = jnp.bfloat16


def _rms(x, g):
    return x * lax.rsqrt(jnp.mean(x * x, axis=-1, keepdims=True) + EPS) * g


def _dot(a, b):
    return jnp.dot(a, b, preferred_element_type=f32)


def _layer_spec(arr, j):
    nd = arr.ndim
    return pl.BlockSpec((None,) + arr.shape[1:], lambda i: (j,) + (0,) * (nd - 1),
                        pipeline_mode=pl.Buffered(1))


def _rows(r, d):
    return pl.BlockSpec((r, d), lambda i: (i, 0))


def _batch_rows(tq, d):
    return pl.BlockSpec((BATCH, tq, d), lambda i: (0, i, 0))


def _params():
    return pltpu.CompilerParams(dimension_semantics=("arbitrary",),
                                vmem_limit_bytes=VMEM_LIMIT_BYTES)


def _lam_power(m, a_re, a_im, step):
    mag = jnp.exp((m * a_re) * step)
    ang = (m * a_im) * step
    return mag * jnp.cos(ang), mag * jnp.sin(ang)


def _s5_matrix_kernel(lsr_ref, arr_ref, air_ref, btre_ref, btim_ref, cre_ref, cim_ref,
                      lsc_ref, arc_ref, aic_ref, e_ref, q_ref, t_ref, lre_ref, lim_ref):
    rows, p = SLAB_GROUPS * SSM_GROUP, SSM_STATE
    step = jnp.exp(lsr_ref[0])
    a_re = arr_ref[0]
    a_im = air_ref[0]
    power = lambda m: _lam_power(m, a_re, a_im, step)

    l_re, l_im = power(1)
    n_re = l_re - 1.0
    den = a_re * a_re + a_im * a_im
    k_re = (n_re * a_re + l_im * a_im) / den
    k_im = (l_im * a_re - n_re * a_im) / den
    bt_re = btre_ref[0]
    bt_im = btim_ref[0]
    bb_re = k_re * bt_re - k_im * bt_im
    bb_im = k_re * bt_im + k_im * bt_re
    c_re = cre_ref[0]
    c_im = cim_ref[0]

    col = lax.broadcasted_iota(jnp.int32, (p, SLAB_STATES), 1)
    spread = (lax.bitwise_and(col, p - 1) == lax.broadcasted_iota(jnp.int32, (p, SLAB_STATES), 0)
              ).astype(f32)
    same_group = (
        lax.shift_right_logical(lax.broadcasted_iota(jnp.int32, (rows, SLAB_STATES), 0),
                                SSM_GROUP.bit_length() - 1)
        == lax.shift_right_logical(lax.broadcasted_iota(jnp.int32, (rows, SLAB_STATES), 1),
                                   SSM_STATE.bit_length() - 1))

    def tile(v):
        wide = jnp.dot(v, spread, precision=lax.Precision.HIGHEST, preferred_element_type=f32)
        return jnp.where(same_group, wide, 0.0)

    def tile_bf16(v):
        wide = _dot(v.astype(bf16), spread.astype(bf16))
        return jnp.where(same_group, wide, 0.0).astype(bf16)

    def dot_nt(a, b):
        return lax.dot_general(a, b, (((1,), (1,)), ((), ())),
                               precision=lax.Precision.HIGHEST, preferred_element_type=f32)

    bb_re_t, bb_im_t = tile(bb_re), tile(bb_im)
    t_blocks = []
    for m in range(SSM_Q + 1):
        p_re, p_im = power(m)
        cp_re_t = tile(c_re * p_re - c_im * p_im)
        cp_im_t = tile(c_re * p_im + c_im * p_re)
        if m < SSM_Q:
            i = SSM_Q - 1 - m
            e_ref[0, i * rows:(i + 1) * rows, 0:SLAB_STATES] = tile_bf16(p_re * bb_re - p_im * bb_im)
            e_ref[0, i * rows:(i + 1) * rows, SLAB_STATES:] = tile_bf16(p_re * bb_im + p_im * bb_re)
            t_blocks.append(dot_nt(bb_re_t, cp_re_t) - dot_nt(bb_im_t, cp_im_t))
        if m >= 1:
            j = m - 1
            q_ref[0, 0:SLAB_STATES, j * rows:(j + 1) * rows] = cp_re_t.T.astype(bf16)
            q_ref[0, SLAB_STATES:, j * rows:(j + 1) * rows] = (-cp_im_t).T.astype(bf16)
    for i in range(SSM_Q):
        for j in range(SSM_Q):
            blk = t_blocks[j - i] if j >= i else jnp.zeros((rows, rows), f32)
            t_ref[0, i * rows:(i + 1) * rows, j * rows:(j + 1) * rows] = blk.astype(bf16)

    lq_re, lq_im = _lam_power(SSM_Q, arc_ref[0], aic_ref[0], jnp.exp(lsc_ref[0]))
    lre_ref[0] = lq_re
    lim_ref[0] = lq_im


def _s5_matrices(log_step, a_re, a_im, b_re, b_im, c_re, c_im):
    nl = log_step.shape[0]
    n = nl * N_SLABS
    rows, p, hg = SLAB_GROUPS * SSM_GROUP, SSM_STATE, SSM_GROUP
    per_row = lambda v: jnp.repeat(v, hg, axis=1).reshape(n, rows, -1)
    by_row = lambda v: v.reshape(n, rows, p)
    compact = lambda v: v.reshape(n, SLAB_GROUPS, -1)

    def spec(*dims):
        return pl.BlockSpec((1,) + dims, lambda i: (i,) + (0,) * len(dims))

    wide = 2 * SLAB_STATES
    emat, qmat, tmat, l_re, l_im = pl.pallas_call(
        _s5_matrix_kernel,
        grid=(n,),
        in_specs=[spec(rows, 1), spec(rows, p), spec(rows, p), spec(rows, p), spec(rows, p),
                  spec(rows, p), spec(rows, p), spec(SLAB_GROUPS, 1), spec(SLAB_GROUPS, p),
                  spec(SLAB_GROUPS, p)],
        out_specs=(spec(SSM_Q * rows, wide), spec(wide, SSM_Q * rows), spec(SSM_Q * rows, SSM_Q * rows),
                   spec(SLAB_GROUPS, p), spec(SLAB_GROUPS, p)),
        out_shape=(jax.ShapeDtypeStruct((n, SSM_Q * rows, wide), bf16),
                   jax.ShapeDtypeStruct((n, wide, SSM_Q * rows), bf16),
                   jax.ShapeDtypeStruct((n, SSM_Q * rows, SSM_Q * rows), bf16),
                   jax.ShapeDtypeStruct((n, SLAB_GROUPS, p), f32),
                   jax.ShapeDtypeStruct((n, SLAB_GROUPS, p), f32)),
        compiler_params=_params(),
        name="s5_matrices",
    )(per_row(log_step[:, :, None]), per_row(a_re), per_row(a_im),
      by_row(jnp.swapaxes(b_re, 2, 3)), by_row(jnp.swapaxes(b_im, 2, 3)), by_row(c_re), by_row(c_im),
      compact(log_step), compact(a_re), compact(a_im))
    split = lambda m: m.reshape((nl, N_SLABS) + m.shape[1:])
    lam = lambda v: v.reshape(nl, 1, N_SSM_GROUPS * SSM_STATE)
    return split(emat), split(qmat), split(tmat), lam(l_re), lam(l_im)


def _even_kernel(x_ref, g_ref, win_ref, cw_ref, lr_ref, li_ref, em_ref, qm_ref, tm_ref,
                 dsk_ref, gw_ref, gb_ref, wout_ref, o_ref, cx_ref, bu_ref, st_ref,
                 xt_ref=None):
    i = pl.program_id(0)
    tile_rows = o_ref.shape[0]

    @pl.when(i == 0)
    def _():
        cx_ref[0:CONV_HALO, :] = jnp.zeros((CONV_HALO, D_CONV), f32)
        st_ref[...] = jnp.zeros_like(st_ref)

    if xt_ref is not None:
        for b in range(BATCH):
            for c in range(D_MODEL // LANES):
                xt_ref[c, pl.ds(b, tile_rows // BATCH, stride=BATCH), :] = (
                    x_ref[b, :, c * LANES:(c + 1) * LANES])

    rows = EVEN_SUB_TQ * BATCH
    tq = EVEN_SUB_TQ
    n_sub = tile_rows // rows
    assert D_CONV == D_SSM and N_SLABS == 4

    def normed(sub):
        r_lo = sub * rows
        if xt_ref is None:
            x = x_ref[r_lo:r_lo + rows, :]
        else:
            x = jnp.concatenate(
                [xt_ref[c, r_lo:r_lo + rows, :] for c in range(D_MODEL // LANES)], axis=1)
        return x, _rms(x, g_ref[...]).astype(bf16)

    def in_slice(h, q):
        return _dot(h, win_ref[:, q * D_CONV:(q + 1) * D_CONV])

    x, h = normed(0)
    xa, ba, ca, u = [in_slice(h, q) for q in range(4)]
    ya = _even_conv(xa, ba, ca, cw_ref, cx_ref)
    u4 = _fold_time(u)
    for s in range(N_SLABS):
        _s5_expand(u4[s], s, em_ref, bu_ref)

    for sub in range(n_sub):
        last = sub + 1 == n_sub
        if not last:
            x_n, h_n = normed(sub + 1)
            parts_n = []
        y4 = []
        for s in range(N_SLABS):
            _s5_scan(s, tq // SSM_Q, lr_ref, li_ref, bu_ref, st_ref)
            if not last:
                parts_n.append(in_slice(h_n, s))
            y4.append(_s5_readout(s, u4[s], qm_ref, tm_ref, bu_ref))
        y = jax.nn.gelu(_unfold_time(y4) + dsk_ref[...] * u)
        if not last:
            ya_n = _even_conv(parts_n[0], parts_n[1], parts_n[2], cw_ref, cx_ref)
            u_n = parts_n[3]
            u4_n = _fold_time(u_n)
            for s in range(N_SLABS // 2):
                _s5_expand(u4_n[s], s, em_ref, bu_ref)
        gate = _dot(y.astype(bf16), gw_ref[...]) + gb_ref[...]
        if not last:
            for s in range(N_SLABS // 2, N_SLABS):
                _s5_expand(u4_n[s], s, em_ref, bu_ref)
        yb = y * jax.nn.sigmoid(gate)
        mix = _dot(ya.astype(bf16), wout_ref[0:D_CONV, :]) + _dot(yb.astype(bf16), wout_ref[D_CONV:, :])
        o_ref[sub * rows:(sub + 1) * rows, :] = x + mix
        if not last:
            x, ya, u, u4 = x_n, ya_n, u_n, u4_n


def _even_conv(xa, ba, ca, cw_ref, cx_ref):
    rows = xa.shape[0]
    cx_ref[CONV_HALO:CONV_HALO + rows, :] = ca * xa
    conv = cx_ref[0:rows, :] * cw_ref[0:1, :]
    for k in range(1, CONV_WIDTH):
        conv = conv + cx_ref[k * BATCH:k * BATCH + rows, :] * cw_ref[k:k + 1, :]
    ya = ba * conv
    cx_ref[0:CONV_HALO, :] = cx_ref[rows:rows + CONV_HALO, :]
    return ya


def _fold_time(u):
    n_chunks = u.shape[0] // (SSM_Q * BATCH)
    at_offset = [
        jnp.concatenate([u[(SSM_Q * k + i) * BATCH:(SSM_Q * k + i + 1) * BATCH, :]
                         for k in range(n_chunks)], axis=0)
        for i in range(SSM_Q)]
    return [jnp.concatenate([a[:, s * LANES:(s + 1) * LANES] for a in at_offset], axis=1).astype(bf16)
            for s in range(N_SLABS)]


def _unfold_time(y4):
    n_chunks = y4[0].shape[0] // BATCH
    at_offset = [jnp.concatenate([y[:, j * LANES:(j + 1) * LANES] for y in y4], axis=1)
                 for j in range(SSM_Q)]
    return jnp.concatenate([at_offset[j][k * BATCH:(k + 1) * BATCH, :]
                            for k in range(n_chunks) for j in range(SSM_Q)], axis=0)


def _s5_expand(u4s, s, em_ref, bu_ref):
    c0 = 2 * SLAB_STATES * s
    bu_ref[:, c0:c0 + 2 * SLAB_STATES] = _dot(u4s, em_ref[s])


def _s5_scan(s, n_chunks, lr_ref, li_ref, bu_ref, st_ref):
    c_re = 2 * SLAB_STATES * s
    c_im = c_re + SLAB_STATES
    re = slice(c_re, c_re + SLAB_STATES)
    im = slice(c_im, c_im + SLAB_STATES)
    lam = slice(SLAB_STATES * s, SLAB_STATES * (s + 1))
    ar = jnp.broadcast_to(lr_ref[:, lam], (BATCH, SLAB_STATES))
    ai = jnp.broadcast_to(li_ref[:, lam], (BATCH, SLAB_STATES))
    s_re = st_ref[:, re]
    s_im = st_ref[:, im]
    for k in range(n_chunks):
        r = slice(k * BATCH, (k + 1) * BATCH)
        b_re = bu_ref[r, re]
        b_im = bu_ref[r, im]
        bu_ref[r, re] = s_re
        bu_ref[r, im] = s_im
        s_re, s_im = ar * s_re - ai * s_im + b_re, ar * s_im + ai * s_re + b_im
    st_ref[:, re] = s_re
    st_ref[:, im] = s_im


def _s5_readout(s, u4s, qm_ref, tm_ref, bu_ref):
    c_re = 2 * SLAB_STATES * s
    c_im = c_re + SLAB_STATES
    return (_dot(bu_ref[:, c_re:c_re + SLAB_STATES].astype(bf16), qm_ref[s, 0:SLAB_STATES, :])
            + _dot(bu_ref[:, c_im:c_im + SLAB_STATES].astype(bf16), qm_ref[s, SLAB_STATES:, :])
            + _dot(u4s, tm_ref[s]))


def _even_mixer(x, j, layer_params, batch_major_in):
    tq = EVEN_TQ_BATCH_MAJOR if batch_major_in else EVEN_TQ
    rows = tq * BATCH
    sub_rows = EVEN_SUB_TQ * BATCH
    x_spec = _batch_rows(tq, D_MODEL) if batch_major_in else _rows(rows, D_MODEL)
    scratch = [
        pltpu.VMEM((sub_rows + CONV_HALO, D_CONV), f32),
        pltpu.VMEM((sub_rows // SSM_Q, SSM_COLS), f32),
        pltpu.VMEM((BATCH, SSM_COLS), f32),
    ]
    if batch_major_in:
        scratch.append(pltpu.VMEM((D_MODEL // LANES, rows, LANES), f32))
    return pl.pallas_call(
        _even_kernel,
        grid=(N_ROWS // rows,),
        in_specs=[x_spec] + [_layer_spec(a, j) for a in layer_params],
        out_specs=_rows(rows, D_MODEL),
        out_shape=jax.ShapeDtypeStruct((N_ROWS, D_MODEL), f32),
        scratch_shapes=scratch,
        compiler_params=_params(),
        name="even_mixer",
    )(x, *layer_params)


def _odd_kernel(x_ref, g_ref, win_ref, pw_ref, ps_ref, ng_ref, sw_ref, sb_ref, wout_ref,
                o_ref, z_ref, v_ref, m_ref):
    i = pl.program_id(0)
    rows = x_ref.shape[0]
    tq = rows // BATCH
    assert tq == CHUNK

    @pl.when(i == 0)
    def _():
        z_ref[0:POOL_HALO, :] = jnp.zeros((POOL_HALO, D_POOL), f32)

    x = x_ref[...]
    h = _rms(x, g_ref[...]).astype(bf16)
    proj = _dot(h, win_ref[...])
    z = proj[:, 0:D_POOL]
    uv = jax.nn.gelu(proj[:, D_POOL:])
    su = uv[:, 0:D_SGU]
    sv = uv[:, D_SGU:]

    z_ref[POOL_HALO:POOL_HALO + rows, :] = z
    count = (i * tq + 1 + lax.broadcasted_iota(jnp.int32, (rows, 1), 0) // BATCH).astype(f32)
    yc = []
    for gi, win in enumerate(POOL_WINDOWS):
        cols = slice(gi * POOL_GROUP, (gi + 1) * POOL_GROUP)
        e = z_ref[:, cols]
        span = 1
        while span < win:
            sh = span * BATCH
            e = e[sh:, :] + e[:-sh, :]
            span *= 2
        wsum = e[e.shape[0] - rows:, :]
        pooled = wsum / jnp.minimum(count, float(win)) - z[:, cols]
        yc.append(_dot(pooled.astype(bf16), pw_ref[gi]))
    yc = jnp.concatenate(yc, axis=1) * ps_ref[...]
    z_ref[0:POOL_HALO, :] = z_ref[rows:rows + POOL_HALO, :]

    v = _rms(sv, ng_ref[...])
    tri = lax.broadcasted_iota(jnp.int32, (CHUNK, CHUNK), 0) >= lax.broadcasted_iota(
        jnp.int32, (CHUNK, CHUNK), 1)
    for hd in range(SGU_HEADS):
        v_ref[hd] = v[:, hd * SGU_HEAD_DIM:(hd + 1) * SGU_HEAD_DIM]
    for hd in range(SGU_HEADS):
        w_s = jnp.where(tri, sw_ref[hd], 0.0).astype(bf16)
        v_b = jnp.concatenate(
            [v_ref[hd, pl.ds(b, CHUNK, stride=BATCH), :].astype(bf16) for b in range(BATCH)],
            axis=1)
        mixed_b = _dot(w_s, v_b) + sb_ref[:, hd:hd + 1]
        for b in range(BATCH):
            m_ref[hd, pl.ds(b, CHUNK, stride=BATCH), :] = mixed_b[:, b * SGU_HEAD_DIM:(b + 1) * SGU_HEAD_DIM]
    yd = su * jnp.concatenate([m_ref[hd] for hd in range(SGU_HEADS)], axis=1)

    mix = _dot(yc.astype(bf16), wout_ref[0:D_POOL, :]) + _dot(yd.astype(bf16), wout_ref[D_POOL:, :])
    o_ref[...] = x + mix


def _odd_mixer(x, j, layer_params):
    rows = ODD_TQ * BATCH
    return pl.pallas_call(
        _odd_kernel,
        grid=(N_ROWS // rows,),
        in_specs=[_rows(rows, D_MODEL)] + [_layer_spec(a, j) for a in layer_params],
        out_specs=_rows(rows, D_MODEL),
        out_shape=jax.ShapeDtypeStruct((N_ROWS, D_MODEL), f32),
        scratch_shapes=[
            pltpu.VMEM((rows + POOL_HALO, D_POOL), f32),
            pltpu.VMEM((SGU_HEADS, rows, SGU_HEAD_DIM), f32),
            pltpu.VMEM((SGU_HEADS, rows, SGU_HEAD_DIM), f32),
        ],
        compiler_params=_params(),
        name="odd_mixer",
    )(x, *layer_params)


def _ffn_chunks():
    out, c0 = [], 0
    while c0 < D_FF:
        cw = min(512, D_FF - c0)
        out.append((c0, cw))
        c0 += cw
    return out


def _ffn_kernel(x_ref, g_ref, wup_ref, cw_ref, cb_ref, wdn_ref, gf_ref, o_ref, up_ref, act_ref,
                yt_ref=None):
    i = pl.program_id(0)
    rows = x_ref.shape[0]
    tq = rows // BATCH

    @pl.when(i == 0)
    def _():
        up_ref[...] = jnp.zeros_like(up_ref)

    x = x_ref[...]
    h = _rms(x, g_ref[...]).astype(bf16)

    def conv(off, cw):
        cols = slice(off, off + cw)
        up = _dot(h, wup_ref[:, cols])
        ext = jnp.concatenate([up_ref[:, cols], up], axis=0)
        up_ref[:, cols] = up[rows - CONV_HALO:rows, :]
        y = ext[0:rows, :] * cw_ref[0:1, cols]
        for k in range(1, CONV_WIDTH):
            y = y + ext[k * BATCH:k * BATCH + rows, :] * cw_ref[k:k + 1, cols]
        return y + cb_ref[:, cols]

    for c0, cw in _ffn_chunks():
        gate = conv(c0, cw)
        val = conv(D_FF + c0, cw)
        act_ref[:, c0:c0 + cw] = (jax.nn.silu(gate) * val).astype(bf16)

    out = x + _dot(act_ref[...], wdn_ref[...])
    if yt_ref is None:
        o_ref[...] = out
    else:
        out = _rms(out, gf_ref[...])
        for c in range(D_MODEL // LANES):
            yt_ref[c] = out[:, c * LANES:(c + 1) * LANES]
        for b in range(BATCH):
            for c in range(D_MODEL // LANES):
                o_ref[b, :, c * LANES:(c + 1) * LANES] = yt_ref[c, pl.ds(b, tq, stride=BATCH), :]


def _conv_ffn(x, j, layer_params, g_final, final):
    rows = FFN_TQ * BATCH
    scratch = [pltpu.VMEM((CONV_HALO, 2 * D_FF), f32), pltpu.VMEM((rows, D_FF), bf16)]
    if final:
        scratch.append(pltpu.VMEM((D_MODEL // LANES, rows, LANES), f32))
        out_spec = _batch_rows(FFN_TQ, D_MODEL)
        out_shape = jax.ShapeDtypeStruct((BATCH, SEQ, D_MODEL), f32)
    else:
        out_spec = _rows(rows, D_MODEL)
        out_shape = jax.ShapeDtypeStruct((N_ROWS, D_MODEL), f32)
    return pl.pallas_call(
        _ffn_kernel,
        grid=(N_ROWS // rows,),
        in_specs=([_rows(rows, D_MODEL)] + [_layer_spec(a, j) for a in layer_params]
                  + [_layer_spec(g_final, 0)]),
        out_specs=out_spec,
        out_shape=out_shape,
        scratch_shapes=scratch,
        compiler_params=_params(),
        name="conv_ffn",
    )(x, *layer_params, g_final)


def kernel(x, norm_mix_g, even_w_in, even_conv_w, ssm_log_step, ssm_a_re, ssm_a_im, ssm_b_re, ssm_b_im, ssm_c_re, ssm_c_im, ssm_d, ssm_glu_w, ssm_glu_b, even_w_out, odd_w_in, pool_w, pool_scale, sgu_norm_g, sgu_w, sgu_b, odd_w_out, norm_ffn_g, ffn_w_up, ffn_conv_w, ffn_conv_b, ffn_w_down, norm_final_g):
    assert x.shape == (BATCH, SEQ, D_MODEL)
    n_even, n_odd = even_w_in.shape[0], odd_w_in.shape[0]
    rows3 = lambda v: v.reshape(v.shape[0], 1, -1)

    emat, qmat, tmat, l_re, l_im = _s5_matrices(
        ssm_log_step, ssm_a_re, ssm_a_im, ssm_b_re, ssm_b_im, ssm_c_re, ssm_c_im)
    even_params = (
        rows3(norm_mix_g[0::2]), even_w_in.astype(bf16), even_conv_w,
        l_re, l_im, emat, qmat, tmat, rows3(ssm_d),
        ssm_glu_w.astype(bf16), rows3(ssm_glu_b), even_w_out.astype(bf16))

    odd_params = (
        rows3(norm_mix_g[1::2]), odd_w_in.astype(bf16), pool_w.astype(bf16),
        rows3(pool_scale), rows3(sgu_norm_g), sgu_w, jnp.swapaxes(sgu_b, 1, 2),
        odd_w_out.astype(bf16))

    ffn_params = (
        rows3(norm_ffn_g), ffn_w_up.astype(bf16), ffn_conv_w, rows3(ffn_conv_b),
        ffn_w_down.astype(bf16))
    g_final = norm_final_g.reshape(1, 1, D_MODEL)

    xt = x
    for i in range(DEPTH):
        j = i // 2
        if i % 2 == 0:
            xt = _even_mixer(xt, j, even_params, batch_major_in=(i == 0))
        else:
            xt = _odd_mixer(xt, j, odd_params)
        xt = _conv_ffn(xt, i, ffn_params, g_final, final=(i == DEPTH - 1))
    return xt
```

```python
import jax
import jax.numpy as jnp
from jax import lax
from jax.experimental import pallas as pl
from jax.experimental.pallas import tpu as pltpu

D_MODEL = 1024
BATCH = 8
SEQ = 4096
DEPTH = 4
D_CONV = D_MODEL // 2
CONV_WIDTH = 3
D_SSM = D_MODEL // 2
SSM_GROUP = 16
N_SSM_GROUPS = D_SSM // SSM_GROUP
SSM_STATE = 64
D_POOL = D_MODEL // 2
POOL_WINDOWS = (2, 4, 8, 16)
POOL_GROUP = D_POOL // len(POOL_WINDOWS)
D_SGU = D_MODEL // 2
SGU_HEADS = 4
SGU_HEAD_DIM = D_SGU // SGU_HEADS
CHUNK = 128
D_FF = ((8 * D_MODEL // 3 + 127) // 128) * 128
EPS = 1e-6

N_ROWS = SEQ * BATCH
LANES = 128
V7X_VMEM_BYTES = 64 * 1024 * 1024
VMEM_LIMIT_BYTES = V7X_VMEM_BYTES - 8 * 1024 * 1024

CONV_HALO = (CONV_WIDTH - 1) * BATCH
POOL_HALO = max(POOL_WINDOWS) * BATCH
SSM_Q = 4
SLAB_GROUPS = LANES // SSM_GROUP
N_SLABS = N_SSM_GROUPS // SLAB_GROUPS
SLAB_STATES = SLAB_GROUPS * SSM_STATE
SSM_COLS = 2 * N_SLABS * SLAB_STATES

EVEN_TQ = 128
EVEN_TQ_BATCH_MAJOR = 128
EVEN_SUB_TQ = 64
FFN_TQ = 128
ODD_TQ = CHUNK

f32 = jnp.float32
bf16 = jnp.bfloat16


def _rms(x, g):
    return x * lax.rsqrt(jnp.mean(x * x, axis=-1, keepdims=True) + EPS) * g


def _dot(a, b):
    return jnp.dot(a, b, preferred_element_type=f32)


def _layer_spec(arr, j):
    nd = arr.ndim
    return pl.BlockSpec((None,) + arr.shape[1:], lambda i: (j,) + (0,) * (nd - 1),
                        pipeline_mode=pl.Buffered(1))


def _rows(r, d):
    return pl.BlockSpec((r, d), lambda i: (i, 0))


def _batch_rows(tq, d):
    return pl.BlockSpec((BATCH, tq, d), lambda i: (0, i, 0))


def _params():
    return pltpu.CompilerParams(dimension_semantics=("arbitrary",),
                                vmem_limit_bytes=VMEM_LIMIT_BYTES)


def _lam_power(m, a_re, a_im, step):
    mag = jnp.exp((m * a_re) * step)
    ang = (m * a_im) * step
    return mag * jnp.cos(ang), mag * jnp.sin(ang)


def _s5_matrix_kernel(lsr_ref, arr_ref, air_ref, btre_ref, btim_ref, cre_ref, cim_ref,
                      lsc_ref, arc_ref, aic_ref, e_ref, q_ref, t_ref, lre_ref, lim_ref):
    rows, p = SLAB_GROUPS * SSM_GROUP, SSM_STATE
    step = jnp.exp(lsr_ref[0])
    a_re = arr_ref[0]
    a_im = air_ref[0]
    power = lambda m: _lam_power(m, a_re, a_im, step)

    l_re, l_im = power(1)
    n_re = l_re - 1.0
    den = a_re * a_re + a_im * a_im
    k_re = (n_re * a_re + l_im * a_im) / den
    k_im = (l_im * a_re - n_re * a_im) / den
    bt_re = btre_ref[0]
    bt_im = btim_ref[0]
    bb_re = k_re * bt_re - k_im * bt_im
    bb_im = k_re * bt_im + k_im * bt_re
    c_re = cre_ref[0]
    c_im = cim_ref[0]

    col = lax.broadcasted_iota(jnp.int32, (p, SLAB_STATES), 1)
    spread = (lax.bitwise_and(col, p - 1) == lax.broadcasted_iota(jnp.int32, (p, SLAB_STATES), 0)
              ).astype(f32)
    same_group = (
        lax.shift_right_logical(lax.broadcasted_iota(jnp.int32, (rows, SLAB_STATES), 0),
                                SSM_GROUP.bit_length() - 1)
        == lax.shift_right_logical(lax.broadcasted_iota(jnp.int32, (rows, SLAB_STATES), 1),
                                   SSM_STATE.bit_length() - 1))

    def tile(v):
        wide = jnp.dot(v, spread, precision=lax.Precision.HIGHEST, preferred_element_type=f32)
        return jnp.where(same_group, wide, 0.0)

    def tile_bf16(v):
        wide = _dot(v.astype(bf16), spread.astype(bf16))
        return jnp.where(same_group, wide, 0.0).astype(bf16)

    def dot_nt(a, b):
        return lax.dot_general(a, b, (((1,), (1,)), ((), ())),
                               precision=lax.Precision.HIGHEST, preferred_element_type=f32)

    bb_re_t, bb_im_t = tile(bb_re), tile(bb_im)
    t_blocks = []
    for m in range(SSM_Q + 1):
        p_re, p_im = power(m)
        cp_re_t = tile(c_re * p_re - c_im * p_im)
        cp_im_t = tile(c_re * p_im + c_im * p_re)
        if m < SSM_Q:
            i = SSM_Q - 1 - m
            e_ref[0, i * rows:(i + 1) * rows, 0:SLAB_STATES] = tile_bf16(p_re * bb_re - p_im * bb_im)
            e_ref[0, i * rows:(i + 1) * rows, SLAB_STATES:] = tile_bf16(p_re * bb_im + p_im * bb_re)
            t_blocks.append(dot_nt(bb_re_t, cp_re_t) - dot_nt(bb_im_t, cp_im_t))
        if m >= 1:
            j = m - 1
            q_ref[0, 0:SLAB_STATES, j * rows:(j + 1) * rows] = cp_re_t.T.astype(bf16)
            q_ref[0, SLAB_STATES:, j * rows:(j + 1) * rows] = (-cp_im_t).T.astype(bf16)
    for i in range(SSM_Q):
        for j in range(SSM_Q):
            blk = t_blocks[j - i] if j >= i else jnp.zeros((rows, rows), f32)
            t_ref[0, i * rows:(i + 1) * rows, j * rows:(j + 1) * rows] = blk.astype(bf16)

    lq_re, lq_im = _lam_power(SSM_Q, arc_ref[0], aic_ref[0], jnp.exp(lsc_ref[0]))
    lre_ref[0] = lq_re
    lim_ref[0] = lq_im


def _s5_matrices(log_step, a_re, a_im, b_re, b_im, c_re, c_im):
    nl = log_step.shape[0]
    n = nl * N_SLABS
    rows, p, hg = SLAB_GROUPS * SSM_GROUP, SSM_STATE, SSM_GROUP
    per_row = lambda v: jnp.repeat(v, hg, axis=1).reshape(n, rows, -1)
    by_row = lambda v: v.reshape(n, rows, p)
    compact = lambda v: v.reshape(n, SLAB_GROUPS, -1)

    def spec(*dims):
        return pl.BlockSpec((1,) + dims, lambda i: (i,) + (0,) * len(dims))

    wide = 2 * SLAB_STATES
    emat, qmat, tmat, l_re, l_im = pl.pallas_call(
        _s5_matrix_kernel,
        grid=(n,),
        in_specs=[spec(rows, 1), spec(rows, p), spec(rows, p), spec(rows, p), spec(rows, p),
                  spec(rows, p), spec(rows, p), spec(SLAB_GROUPS, 1), spec(SLAB_GROUPS, p),
                  spec(SLAB_GROUPS, p)],
        out_specs=(spec(SSM_Q * rows, wide), spec(wide, SSM_Q * rows), spec(SSM_Q * rows, SSM_Q * rows),
                   spec(SLAB_GROUPS, p), spec(SLAB_GROUPS, p)),
        out_shape=(jax.ShapeDtypeStruct((n, SSM_Q * rows, wide), bf16),
                   jax.ShapeDtypeStruct((n, wide, SSM_Q * rows), bf16),
                   jax.ShapeDtypeStruct((n, SSM_Q * rows, SSM_Q * rows), bf16),
                   jax.ShapeDtypeStruct((n, SLAB_GROUPS, p), f32),
                   jax.ShapeDtypeStruct((n, SLAB_GROUPS, p), f32)),
        compiler_params=_params(),
        name="s5_matrices",
    )(per_row(log_step[:, :, None]), per_row(a_re), per_row(a_im),
      by_row(jnp.swapaxes(b_re, 2, 3)), by_row(jnp.swapaxes(b_im, 2, 3)), by_row(c_re), by_row(c_im),
      compact(log_step), compact(a_re), compact(a_im))
    split = lambda m: m.reshape((nl, N_SLABS) + m.shape[1:])
    lam = lambda v: v.reshape(nl, 1, N_SSM_GROUPS * SSM_STATE)
    return split(emat), split(qmat), split(tmat), lam(l_re), lam(l_im)


def _even_kernel(x_ref, g_ref, win_ref, cw_ref, lr_ref, li_ref, em_ref, qm_ref, tm_ref,
                 dsk_ref, gw_ref, gb_ref, wout_ref, o_ref, cx_ref, bu_ref, st_ref,
                 xt_ref=None):
    i = pl.program_id(0)
    tile_rows = o_ref.shape[0]

    @pl.when(i == 0)
    def _():
        cx_ref[0:CONV_HALO, :] = jnp.zeros((CONV_HALO, D_CONV), f32)
        st_ref[...] = jnp.zeros_like(st_ref)

    if xt_ref is not None:
        for b in range(BATCH):
            for c in range(D_MODEL // LANES):
                xt_ref[c, pl.ds(b, tile_rows // BATCH, stride=BATCH), :] = (
                    x_ref[b, :, c * LANES:(c + 1) * LANES])

    rows = EVEN_SUB_TQ * BATCH
    tq = EVEN_SUB_TQ
    n_sub = tile_rows // rows
    assert D_CONV == D_SSM and N_SLABS == 4

    def normed(sub):
        r_lo = sub * rows
        if xt_ref is None:
            x = x_ref[r_lo:r_lo + rows, :]
        else:
            x = jnp.concatenate(
                [xt_ref[c, r_lo:r_lo + rows, :] for c in range(D_MODEL // LANES)], axis=1)
        return x, _rms(x, g_ref[...]).astype(bf16)

    def in_slice(h, q):
        return _dot(h, win_ref[:, q * D_CONV:(q + 1) * D_CONV])

    x, h = normed(0)
    xa, ba, ca, u = [in_slice(h, q) for q in range(4)]
    ya = _even_conv(xa, ba, ca, cw_ref, cx_ref)
    u4 = _fold_time(u)
    for s in range(N_SLABS):
        _s5_expand(u4[s], s, em_ref, bu_ref)

    for sub in range(n_sub):
        last = sub + 1 == n_sub
        if not last:
            x_n, h_n = normed(sub + 1)
            parts_n = []
        y4 = []
        for s in range(N_SLABS):
            _s5_scan(s, tq // SSM_Q, lr_ref, li_ref, bu_ref, st_ref)
            if not last:
                parts_n.append(in_slice(h_n, s))
            y4.append(_s5_readout(s, u4[s], qm_ref, tm_ref, bu_ref))
        y = jax.nn.gelu(_unfold_time(y4) + dsk_ref[...] * u)
        if not last:
            ya_n = _even_conv(parts_n[0], parts_n[1], parts_n[2], cw_ref, cx_ref)
            u_n = parts_n[3]
            u4_n = _fold_time(u_n)
            for s in range(N_SLABS // 2):
                _s5_expand(u4_n[s], s, em_ref, bu_ref)
        gate = _dot(y.astype(bf16), gw_ref[...]) + gb_ref[...]
        if not last:
            for s in range(N_SLABS // 2, N_SLABS):
                _s5_expand(u4_n[s], s, em_ref, bu_ref)
        yb = y * jax.nn.sigmoid(gate)
        mix = _dot(ya.astype(bf16), wout_ref[0:D_CONV, :]) + _dot(yb.astype(bf16), wout_ref[D_CONV:, :])
        o_ref[sub * rows:(sub + 1) * rows, :] = x + mix
        if not last:
            x, ya, u, u4 = x_n, ya_n, u_n, u4_n


def _even_conv(xa, ba, ca, cw_ref, cx_ref):
    rows = xa.shape[0]
    cx_ref[CONV_HALO:CONV_HALO + rows, :] = ca * xa
    conv = cx_ref[0:rows, :] * cw_ref[0:1, :]
    for k in range(1, CONV_WIDTH):
        conv = conv + cx_ref[k * BATCH:k * BATCH + rows, :] * cw_ref[k:k + 1, :]
    ya = ba * conv
    cx_ref[0:CONV_HALO, :] = cx_ref[rows:rows + CONV_HALO, :]
    return ya


def _fold_time(u):
    n_chunks = u.shape[0] // (SSM_Q * BATCH)
    at_offset = [
        jnp.concatenate([u[(SSM_Q * k + i) * BATCH:(SSM_Q * k + i + 1) * BATCH, :]
                         for k in range(n_chunks)], axis=0)
        for i in range(SSM_Q)]
    return [jnp.concatenate([a[:, s * LANES:(s + 1) * LANES] for a in at_offset], axis=1).astype(bf16)
            for s in range(N_SLABS)]


def _unfold_time(y4):
    n_chunks = y4[0].shape[0] // BATCH
    at_offset = [jnp.concatenate([y[:, j * LANES:(j + 1) * LANES] for y in y4], axis=1)
                 for j in range(SSM_Q)]
    return jnp.concatenate([at_offset[j][k * BATCH:(k + 1) * BATCH, :]
                            for k in range(n_chunks) for j in range(SSM_Q)], axis=0)


def _s5_expand(u4s, s, em_ref, bu_ref):
    c0 = 2 * SLAB_STATES * s
    bu_ref[:, c0:c0 + 2 * SLAB_STATES] = _dot(u4s, em_ref[s])


def _s5_scan(s, n_chunks, lr_ref, li_ref, bu_ref, st_ref):
    c_re = 2 * SLAB_STATES * s
    c_im = c_re + SLAB_STATES
    re = slice(c_re, c_re + SLAB_STATES)
    im = slice(c_im, c_im + SLAB_STATES)
    lam = slice(SLAB_STATES * s, SLAB_STATES * (s + 1))
    ar = jnp.broadcast_to(lr_ref[:, lam], (BATCH, SLAB_STATES))
    ai = jnp.broadcast_to(li_ref[:, lam], (BATCH, SLAB_STATES))
    s_re = st_ref[:, re]
    s_im = st_ref[:, im]
    for k in range(n_chunks):
        r = slice(k * BATCH, (k + 1) * BATCH)
        b_re = bu_ref[r, re]
        b_im = bu_ref[r, im]
        bu_ref[r, re] = s_re
        bu_ref[r, im] = s_im
        s_re, s_im = ar * s_re - ai * s_im + b_re, ar * s_im + ai * s_re + b_im
    st_ref[:, re] = s_re
    st_ref[:, im] = s_im


def _s5_readout(s, u4s, qm_ref, tm_ref, bu_ref):
    c_re = 2 * SLAB_STATES * s
    c_im = c_re + SLAB_STATES
    return (_dot(bu_ref[:, c_re:c_re + SLAB_STATES].astype(bf16), qm_ref[s, 0:SLAB_STATES, :])
            + _dot(bu_ref[:, c_im:c_im + SLAB_STATES].astype(bf16), qm_ref[s, SLAB_STATES:, :])
            + _dot(u4s, tm_ref[s]))


def _even_mixer(x, j, layer_params, batch_major_in):
    tq = EVEN_TQ_BATCH_MAJOR if batch_major_in else EVEN_TQ
    rows = tq * BATCH
    sub_rows = EVEN_SUB_TQ * BATCH
    x_spec = _batch_rows(tq, D_MODEL) if batch_major_in else _rows(rows, D_MODEL)
    scratch = [
        pltpu.VMEM((sub_rows + CONV_HALO, D_CONV), f32),
        pltpu.VMEM((sub_rows // SSM_Q, SSM_COLS), f32),
        pltpu.VMEM((BATCH, SSM_COLS), f32),
    ]
    if batch_major_in:
        scratch.append(pltpu.VMEM((D_MODEL // LANES, rows, LANES), f32))
    return pl.pallas_call(
        _even_kernel,
        grid=(N_ROWS // rows,),
        in_specs=[x_spec] + [_layer_spec(a, j) for a in layer_params],
        out_specs=_rows(rows, D_MODEL),
        out_shape=jax.ShapeDtypeStruct((N_ROWS, D_MODEL), f32),
        scratch_shapes=scratch,
        compiler_params=_params(),
        name="even_mixer",
    )(x, *layer_params)


def _odd_kernel(x_ref, g_ref, win_ref, pw_ref, ps_ref, ng_ref, sw_ref, sb_ref, wout_ref,
                o_ref, z_ref, v_ref, m_ref):
    i = pl.program_id(0)
    rows = x_ref.shape[0]
    tq = rows // BATCH
    assert tq == CHUNK

    @pl.when(i == 0)
    def _():
        z_ref[0:POOL_HALO, :] = jnp.zeros((POOL_HALO, D_POOL), f32)

    x = x_ref[...]
    h = _rms(x, g_ref[...]).astype(bf16)
    proj = _dot(h, win_ref[...])
    z = proj[:, 0:D_POOL]
    uv = jax.nn.gelu(proj[:, D_POOL:])
    su = uv[:, 0:D_SGU]
    sv = uv[:, D_SGU:]

    z_ref[POOL_HALO:POOL_HALO + rows, :] = z
    count = (i * tq + 1 + lax.broadcasted_iota(jnp.int32, (rows, 1), 0) // BATCH).astype(f32)
    yc = []
    for gi, win in enumerate(POOL_WINDOWS):
        cols = slice(gi * POOL_GROUP, (gi + 1) * POOL_GROUP)
        e = z_ref[:, cols]
        span = 1
        while span < win:
            sh = span * BATCH
            e = e[sh:, :] + e[:-sh, :]
            span *= 2
        wsum = e[e.shape[0] - rows:, :]
        pooled = wsum / jnp.minimum(count, float(win)) - z[:, cols]
        yc.append(_dot(pooled.astype(bf16), pw_ref[gi]))
    yc = jnp.concatenate(yc, axis=1) * ps_ref[...]
    z_ref[0:POOL_HALO, :] = z_ref[rows:rows + POOL_HALO, :]

    v = _rms(sv, ng_ref[...])
    tri = lax.broadcasted_iota(jnp.int32, (CHUNK, CHUNK), 0) >= lax.broadcasted_iota(
        jnp.int32, (CHUNK, CHUNK), 1)
    for hd in range(SGU_HEADS):
        v_ref[hd] = v[:, hd * SGU_HEAD_DIM:(hd + 1) * SGU_HEAD_DIM]
    for hd in range(SGU_HEADS):
        w_s = jnp.where(tri, sw_ref[hd], 0.0).astype(bf16)
        v_b = jnp.concatenate(
            [v_ref[hd, pl.ds(b, CHUNK, stride=BATCH), :].astype(bf16) for b in range(BATCH)],
            axis=1)
        mixed_b = _dot(w_s, v_b) + sb_ref[:, hd:hd + 1]
        for b in range(BATCH):
            m_ref[hd, pl.ds(b, CHUNK, stride=BATCH), :] = mixed_b[:, b * SGU_HEAD_DIM:(b + 1) * SGU_HEAD_DIM]
    yd = su * jnp.concatenate([m_ref[hd] for hd in range(SGU_HEADS)], axis=1)

    mix = _dot(yc.astype(bf16), wout_ref[0:D_POOL, :]) + _dot(yd.astype(bf16), wout_ref[D_POOL:, :])
    o_ref[...] = x + mix


def _odd_mixer(x, j, layer_params):
    rows = ODD_TQ * BATCH
    return pl.pallas_call(
        _odd_kernel,
        grid=(N_ROWS // rows,),
        in_specs=[_rows(rows, D_MODEL)] + [_layer_spec(a, j) for a in layer_params],
        out_specs=_rows(rows, D_MODEL),
        out_shape=jax.ShapeDtypeStruct((N_ROWS, D_MODEL), f32),
        scratch_shapes=[
            pltpu.VMEM((rows + POOL_HALO, D_POOL), f32),
            pltpu.VMEM((SGU_HEADS, rows, SGU_HEAD_DIM), f32),
            pltpu.VMEM((SGU_HEADS, rows, SGU_HEAD_DIM), f32),
        ],
        compiler_params=_params(),
        name="odd_mixer",
    )(x, *layer_params)


def _ffn_chunks():
    out, c0 = [], 0
    while c0 < D_FF:
        cw = min(256, D_FF - c0)
        out.append((c0, cw))
        c0 += cw
    return out


def _ffn_kernel(x_ref, g_ref, wup_ref, cw_ref, cb_ref, wdn_ref, gf_ref, o_ref, up_ref, act_ref,
                yt_ref=None):
    i = pl.program_id(0)
    rows = x_ref.shape[0]
    tq = rows // BATCH

    @pl.when(i == 0)
    def _():
        up_ref[...] = jnp.zeros_like(up_ref)

    x = x_ref[...]
    h = _rms(x, g_ref[...]).astype(bf16)

    def conv(off, cw):
        cols = slice(off, off + cw)
        up = _dot(h, wup_ref[:, cols])
        ext = jnp.concatenate([up_ref[:, cols], up], axis=0)
        up_ref[:, cols] = up[rows - CONV_HALO:rows, :]
        y = ext[0:rows, :] * cw_ref[0:1, cols]
        for k in range(1, CONV_WIDTH):
            y = y + ext[k * BATCH:k * BATCH + rows, :] * cw_ref[k:k + 1, cols]
        return y + cb_ref[:, cols]

    for c0, cw in _ffn_chunks():
        gate = conv(c0, cw)
        val = conv(D_FF + c0, cw)
        act_ref[:, c0:c0 + cw] = (jax.nn.silu(gate) * val).astype(bf16)

    out = x + _dot(act_ref[...], wdn_ref[...])
    if yt_ref is None:
        o_ref[...] = out
    else:
        out = _rms(out, gf_ref[...])
        for c in range(D_MODEL // LANES):
            yt_ref[c] = out[:, c * LANES:(c + 1) * LANES]
        for b in range(BATCH):
            for c in range(D_MODEL // LANES):
                o_ref[b, :, c * LANES:(c + 1) * LANES] = yt_ref[c, pl.ds(b, tq, stride=BATCH), :]


def _conv_ffn(x, j, layer_params, g_final, final):
    rows = FFN_TQ * BATCH
    scratch = [pltpu.VMEM((CONV_HALO, 2 * D_FF), f32), pltpu.VMEM((rows, D_FF), bf16)]
    if final:
        scratch.append(pltpu.VMEM((D_MODEL // LANES, rows, LANES), f32))
        out_spec = _batch_rows(FFN_TQ, D_MODEL)
        out_shape = jax.ShapeDtypeStruct((BATCH, SEQ, D_MODEL), f32)
    else:
        out_spec = _rows(rows, D_MODEL)
        out_shape = jax.ShapeDtypeStruct((N_ROWS, D_MODEL), f32)
    return pl.pallas_call(
        _ffn_kernel,
        grid=(N_ROWS // rows,),
        in_specs=([_rows(rows, D_MODEL)] + [_layer_spec(a, j) for a in layer_params]
                  + [_layer_spec(g_final, 0)]),
        out_specs=out_spec,
        out_shape=out_shape,
        scratch_shapes=scratch,
        compiler_params=_params(),
        name="conv_ffn",
    )(x, *layer_params, g_final)


def kernel(x, norm_mix_g, even_w_in, even_conv_w, ssm_log_step, ssm_a_re, ssm_a_im, ssm_b_re, ssm_b_im, ssm_c_re, ssm_c_im, ssm_d, ssm_glu_w, ssm_glu_b, even_w_out, odd_w_in, pool_w, pool_scale, sgu_norm_g, sgu_w, sgu_b, odd_w_out, norm_ffn_g, ffn_w_up, ffn_conv_w, ffn_conv_b, ffn_w_down, norm_final_g):
    assert x.shape == (BATCH, SEQ, D_MODEL)
    n_even, n_odd = even_w_in.shape[0], odd_w_in.shape[0]
    rows3 = lambda v: v.reshape(v.shape[0], 1, -1)

    emat, qmat, tmat, l_re, l_im = _s5_matrices(
        ssm_log_step, ssm_a_re, ssm_a_im, ssm_b_re, ssm_b_im, ssm_c_re, ssm_c_im)
    even_params = (
        rows3(norm_mix_g[0::2]), even_w_in.astype(bf16), even_conv_w,
        l_re, l_im, emat, qmat, tmat, rows3(ssm_d),
        ssm_glu_w.astype(bf16), rows3(ssm_glu_b), even_w_out.astype(bf16))

    odd_params = (
        rows3(norm_mix_g[1::2]), odd_w_in.astype(bf16), pool_w.astype(bf16),
        rows3(pool_scale), rows3(sgu_norm_g), sgu_w, jnp.swapaxes(sgu_b, 1, 2),
        odd_w_out.astype(bf16))

    ffn_params = (
        rows3(norm_ffn_g), ffn_w_up.astype(bf16), ffn_conv_w, rows3(ffn_conv_b),
        ffn_w_down.astype(bf16))
    g_final = norm_final_g.reshape(1, 1, D_MODEL)

    xt = x
    for i in range(DEPTH):
        j = i // 2
        if i % 2 == 0:
            xt = _even_mixer(xt, j, even_params, batch_major_in=(i == 0))
        else:
            xt = _odd_mixer(xt, j, odd_params)
        xt = _conv_ffn(xt, i, ffn_params, g_final, final=(i == DEPTH - 1))
    return xt
```

```python
import functools

import jax
import jax.numpy as jnp
from jax import lax
from jax.experimental import pallas as pl
from jax.experimental.pallas import tpu as pltpu

D_MODEL = 1024
BATCH = 8
SEQ = 4096
DEPTH = 4
D_CONV = D_MODEL // 2
CONV_WIDTH = 3
D_SSM = D_MODEL // 2
SSM_GROUP = 16
N_SSM_GROUPS = D_SSM // SSM_GROUP
SSM_STATE = 64
D_POOL = D_MODEL // 2
POOL_WINDOWS = (2, 4, 8, 16)
POOL_GROUP = D_POOL // len(POOL_WINDOWS)
D_SGU = D_MODEL // 2
SGU_HEADS = 4
SGU_HEAD_DIM = D_SGU // SGU_HEADS
CHUNK = 128
D_FF = ((8 * D_MODEL // 3 + 127) // 128) * 128
EPS = 1e-6

N_ROWS = SEQ * BATCH
LANES = 128
V7X_VMEM_BYTES = 64 * 1024 * 1024
VMEM_LIMIT_BYTES = V7X_VMEM_BYTES - 8 * 1024 * 1024

CONV_HALO = (CONV_WIDTH - 1) * BATCH
POOL_HALO = max(POOL_WINDOWS) * BATCH
SSM_Q = 4
SLAB_GROUPS = LANES // SSM_GROUP
N_SLABS = N_SSM_GROUPS // SLAB_GROUPS
SLAB_STATES = SLAB_GROUPS * SSM_STATE
SSM_COLS = 2 * N_SLABS * SLAB_STATES

EVEN_TQ = 128
EVEN_TQ_BATCH_MAJOR = 128
EVEN_SUB_TQ = 64
FFN_TQ = 128
ODD_TQ = CHUNK

f32 = jnp.float32
bf16 = jnp.bfloat16


def _rms(x, g):
    return x * lax.rsqrt(jnp.mean(x * x, axis=-1, keepdims=True) + EPS) * g


def _dot(a, b):
    return jnp.dot(a, b, preferred_element_type=f32)


def _layer_spec(arr, j):
    nd = arr.ndim
    if nd == 2:
        return pl.BlockSpec(arr.shape, lambda i: (0, 0))
    return pl.BlockSpec((None,) + arr.shape[1:], lambda i: (j,) + (0,) * (nd - 1),
                        pipeline_mode=pl.Buffered(1))


def _rows(r, d):
    return pl.BlockSpec((r, d), lambda i: (i, 0))


def _batch_rows(tq, d):
    return pl.BlockSpec((BATCH, tq, d), lambda i: (0, i, 0))


def _params():
    return pltpu.CompilerParams(dimension_semantics=("arbitrary",),
                                vmem_limit_bytes=VMEM_LIMIT_BYTES)


def _lam_power(m, a_re, a_im, step):
    mag = jnp.exp((m * a_re) * step)
    ang = (m * a_im) * step
    return mag * jnp.cos(ang), mag * jnp.sin(ang)


def _s5_matrix_kernel(lsr_ref, arr_ref, air_ref, btre_ref, btim_ref, cre_ref, cim_ref,
                      lsc_ref, arc_ref, aic_ref, e_ref, q_ref, t_ref, lre_ref, lim_ref):
    rows, p = SLAB_GROUPS * SSM_GROUP, SSM_STATE
    step = jnp.exp(lsr_ref[0])
    a_re = arr_ref[0]
    a_im = air_ref[0]
    power = lambda m: _lam_power(m, a_re, a_im, step)

    l_re, l_im = power(1)
    n_re = l_re - 1.0
    den = a_re * a_re + a_im * a_im
    k_re = (n_re * a_re + l_im * a_im) / den
    k_im = (l_im * a_re - n_re * a_im) / den
    bt_re = btre_ref[0]
    bt_im = btim_ref[0]
    bb_re = k_re * bt_re - k_im * bt_im
    bb_im = k_re * bt_im + k_im * bt_re
    c_re = cre_ref[0]
    c_im = cim_ref[0]

    col = lax.broadcasted_iota(jnp.int32, (p, SLAB_STATES), 1)
    spread = (lax.bitwise_and(col, p - 1) == lax.broadcasted_iota(jnp.int32, (p, SLAB_STATES), 0)
              ).astype(f32)
    same_group = (
        lax.shift_right_logical(lax.broadcasted_iota(jnp.int32, (rows, SLAB_STATES), 0),
                                SSM_GROUP.bit_length() - 1)
        == lax.shift_right_logical(lax.broadcasted_iota(jnp.int32, (rows, SLAB_STATES), 1),
                                   SSM_STATE.bit_length() - 1))

    def tile(v):
        wide = jnp.dot(v, spread, precision=lax.Precision.HIGHEST, preferred_element_type=f32)
        return jnp.where(same_group, wide, 0.0)

    def tile_bf16(v):
        wide = _dot(v.astype(bf16), spread.astype(bf16))
        return jnp.where(same_group, wide, 0.0).astype(bf16)

    def dot_nt(a, b):
        return lax.dot_general(a, b, (((1,), (1,)), ((), ())),
                               precision=lax.Precision.HIGHEST, preferred_element_type=f32)

    bb_re_t, bb_im_t = tile(bb_re), tile(bb_im)
    t_blocks = []
    for m in range(SSM_Q + 1):
        p_re, p_im = power(m)
        cp_re_t = tile(c_re * p_re - c_im * p_im)
        cp_im_t = tile(c_re * p_im + c_im * p_re)
        if m < SSM_Q:
            i = SSM_Q - 1 - m
            e_ref[0, i * rows:(i + 1) * rows, 0:SLAB_STATES] = tile_bf16(p_re * bb_re - p_im * bb_im)
            e_ref[0, i * rows:(i + 1) * rows, SLAB_STATES:] = tile_bf16(p_re * bb_im + p_im * bb_re)
            t_blocks.append(dot_nt(bb_re_t, cp_re_t) - dot_nt(bb_im_t, cp_im_t))
        if m >= 1:
            j = m - 1
            q_ref[0, 0:SLAB_STATES, j * rows:(j + 1) * rows] = cp_re_t.T.astype(bf16)
            q_ref[0, SLAB_STATES:, j * rows:(j + 1) * rows] = (-cp_im_t).T.astype(bf16)
    for i in range(SSM_Q):
        for j in range(SSM_Q):
            blk = t_blocks[j - i] if j >= i else jnp.zeros((rows, rows), f32)
            t_ref[0, i * rows:(i + 1) * rows, j * rows:(j + 1) * rows] = blk.astype(bf16)

    lq_re, lq_im = _lam_power(SSM_Q, arc_ref[0], aic_ref[0], jnp.exp(lsc_ref[0]))
    lre_ref[0] = lq_re
    lim_ref[0] = lq_im


def _s5_matrices(log_step, a_re, a_im, b_re, b_im, c_re, c_im):
    nl = log_step.shape[0]
    n = nl * N_SLABS
    rows, p, hg = SLAB_GROUPS * SSM_GROUP, SSM_STATE, SSM_GROUP
    per_row = lambda v: jnp.repeat(v, hg, axis=1).reshape(n, rows, -1)
    by_row = lambda v: v.reshape(n, rows, p)
    compact = lambda v: v.reshape(n, SLAB_GROUPS, -1)

    def spec(*dims):
        return pl.BlockSpec((1,) + dims, lambda i: (i,) + (0,) * len(dims))

    wide = 2 * SLAB_STATES
    emat, qmat, tmat, l_re, l_im = pl.pallas_call(
        _s5_matrix_kernel,
        grid=(n,),
        in_specs=[spec(rows, 1), spec(rows, p), spec(rows, p), spec(rows, p), spec(rows, p),
                  spec(rows, p), spec(rows, p), spec(SLAB_GROUPS, 1), spec(SLAB_GROUPS, p),
                  spec(SLAB_GROUPS, p)],
        out_specs=(spec(SSM_Q * rows, wide), spec(wide, SSM_Q * rows), spec(SSM_Q * rows, SSM_Q * rows),
                   spec(SLAB_GROUPS, p), spec(SLAB_GROUPS, p)),
        out_shape=(jax.ShapeDtypeStruct((n, SSM_Q * rows, wide), bf16),
                   jax.ShapeDtypeStruct((n, wide, SSM_Q * rows), bf16),
                   jax.ShapeDtypeStruct((n, SSM_Q * rows, SSM_Q * rows), bf16),
                   jax.ShapeDtypeStruct((n, SLAB_GROUPS, p), f32),
                   jax.ShapeDtypeStruct((n, SLAB_GROUPS, p), f32)),
        compiler_params=_params(),
        name="s5_matrices",
    )(per_row(log_step[:, :, None]), per_row(a_re), per_row(a_im),
      by_row(jnp.swapaxes(b_re, 2, 3)), by_row(jnp.swapaxes(b_im, 2, 3)), by_row(c_re), by_row(c_im),
      compact(log_step), compact(a_re), compact(a_im))
    split = lambda m: m.reshape((nl, N_SLABS) + m.shape[1:])
    lam = lambda v: v.reshape(nl, 1, N_SSM_GROUPS * SSM_STATE)
    return split(emat), split(qmat), split(tmat), lam(l_re), lam(l_im)


def _even_kernel(x_ref, g_ref, win_ref, cw_ref, lr_ref, li_ref, em_ref, qm_ref, tm_ref,
                 dsk_ref, gw_ref, gb_ref, wout_ref, o_ref, cx_ref, bu_ref, st_ref,
                 xt_ref=None, *, depth_idx, layer_idx):
    i = pl.program_id(0)
    gain = g_ref[depth_idx:depth_idx + 1, :]
    d_skip = dsk_ref[layer_idx:layer_idx + 1, :]
    glu_b = gb_ref[layer_idx:layer_idx + 1, :]
    tile_rows = o_ref.shape[0]

    @pl.when(i == 0)
    def _():
        cx_ref[0:CONV_HALO, :] = jnp.zeros((CONV_HALO, D_CONV), f32)
        st_ref[...] = jnp.zeros_like(st_ref)

    if xt_ref is not None:
        for b in range(BATCH):
            for c in range(D_MODEL // LANES):
                xt_ref[c, pl.ds(b, tile_rows // BATCH, stride=BATCH), :] = (
                    x_ref[b, :, c * LANES:(c + 1) * LANES])

    rows = EVEN_SUB_TQ * BATCH
    tq = EVEN_SUB_TQ
    n_sub = tile_rows // rows
    assert D_CONV == D_SSM and N_SLABS == 4

    def normed(sub):
        r_lo = sub * rows
        if xt_ref is None:
            x = x_ref[r_lo:r_lo + rows, :]
        else:
            x = jnp.concatenate(
                [xt_ref[c, r_lo:r_lo + rows, :] for c in range(D_MODEL // LANES)], axis=1)
        return x, _rms(x, gain).astype(bf16)

    def in_slice(h, q):
        return _dot(h, win_ref[:, q * D_CONV:(q + 1) * D_CONV])

    x, h = normed(0)
    xa, ba, ca, u = [in_slice(h, q) for q in range(4)]
    ya = _even_conv(xa, ba, ca, cw_ref, cx_ref)
    u4 = _fold_time(u)
    for s in range(N_SLABS):
        _s5_expand(u4[s], s, em_ref, bu_ref)

    for sub in range(n_sub):
        last = sub + 1 == n_sub
        if not last:
            x_n, h_n = normed(sub + 1)
            parts_n = []
        y4 = []
        for s in range(N_SLABS):
            _s5_scan(s, tq // SSM_Q, lr_ref, li_ref, bu_ref, st_ref)
            if not last:
                parts_n.append(in_slice(h_n, s))
            y4.append(_s5_readout(s, u4[s], qm_ref, tm_ref, bu_ref))
        y = jax.nn.gelu(_unfold_time(y4) + d_skip * u)
        if not last:
            ya_n = _even_conv(parts_n[0], parts_n[1], parts_n[2], cw_ref, cx_ref)
            u_n = parts_n[3]
            u4_n = _fold_time(u_n)
            for s in range(N_SLABS // 2):
                _s5_expand(u4_n[s], s, em_ref, bu_ref)
        gate = _dot(y.astype(bf16), gw_ref[...]) + glu_b
        if not last:
            for s in range(N_SLABS // 2, N_SLABS):
                _s5_expand(u4_n[s], s, em_ref, bu_ref)
        yb = y * jax.nn.sigmoid(gate)
        mix = _dot(ya.astype(bf16), wout_ref[0:D_CONV, :]) + _dot(yb.astype(bf16), wout_ref[D_CONV:, :])
        o_ref[sub * rows:(sub + 1) * rows, :] = x + mix
        if not last:
            x, ya, u, u4 = x_n, ya_n, u_n, u4_n


def _even_conv(xa, ba, ca, cw_ref, cx_ref):
    rows = xa.shape[0]
    cx_ref[CONV_HALO:CONV_HALO + rows, :] = ca * xa
    conv = cx_ref[0:rows, :] * cw_ref[0:1, :]
    for k in range(1, CONV_WIDTH):
        conv = conv + cx_ref[k * BATCH:k * BATCH + rows, :] * cw_ref[k:k + 1, :]
    ya = ba * conv
    cx_ref[0:CONV_HALO, :] = cx_ref[rows:rows + CONV_HALO, :]
    return ya


def _fold_time(u):
    n_chunks = u.shape[0] // (SSM_Q * BATCH)
    at_offset = [
        jnp.concatenate([u[(SSM_Q * k + i) * BATCH:(SSM_Q * k + i + 1) * BATCH, :]
                         for k in range(n_chunks)], axis=0)
        for i in range(SSM_Q)]
    return [jnp.concatenate([a[:, s * LANES:(s + 1) * LANES] for a in at_offset], axis=1).astype(bf16)
            for s in range(N_SLABS)]


def _unfold_time(y4):
    n_chunks = y4[0].shape[0] // BATCH
    at_offset = [jnp.concatenate([y[:, j * LANES:(j + 1) * LANES] for y in y4], axis=1)
                 for j in range(SSM_Q)]
    return jnp.concatenate([at_offset[j][k * BATCH:(k + 1) * BATCH, :]
                            for k in range(n_chunks) for j in range(SSM_Q)], axis=0)


def _s5_expand(u4s, s, em_ref, bu_ref):
    c0 = 2 * SLAB_STATES * s
    bu_ref[:, c0:c0 + 2 * SLAB_STATES] = _dot(u4s, em_ref[s])


def _s5_scan(s, n_chunks, lr_ref, li_ref, bu_ref, st_ref):
    c_re = 2 * SLAB_STATES * s
    c_im = c_re + SLAB_STATES
    re = slice(c_re, c_re + SLAB_STATES)
    im = slice(c_im, c_im + SLAB_STATES)
    lam = slice(SLAB_STATES * s, SLAB_STATES * (s + 1))
    ar = jnp.broadcast_to(lr_ref[:, lam], (BATCH, SLAB_STATES))
    ai = jnp.broadcast_to(li_ref[:, lam], (BATCH, SLAB_STATES))
    s_re = st_ref[:, re]
    s_im = st_ref[:, im]
    for k in range(n_chunks):
        r = slice(k * BATCH, (k + 1) * BATCH)
        b_re = bu_ref[r, re]
        b_im = bu_ref[r, im]
        bu_ref[r, re] = s_re
        bu_ref[r, im] = s_im
        s_re, s_im = ar * s_re - ai * s_im + b_re, ar * s_im + ai * s_re + b_im
    st_ref[:, re] = s_re
    st_ref[:, im] = s_im


def _s5_readout(s, u4s, qm_ref, tm_ref, bu_ref):
    c_re = 2 * SLAB_STATES * s
    c_im = c_re + SLAB_STATES
    return (_dot(bu_ref[:, c_re:c_re + SLAB_STATES].astype(bf16), qm_ref[s, 0:SLAB_STATES, :])
            + _dot(bu_ref[:, c_im:c_im + SLAB_STATES].astype(bf16), qm_ref[s, SLAB_STATES:, :])
            + _dot(u4s, tm_ref[s]))


def _even_mixer(x, i, j, layer_params, batch_major_in):
    tq = EVEN_TQ_BATCH_MAJOR if batch_major_in else EVEN_TQ
    rows = tq * BATCH
    sub_rows = EVEN_SUB_TQ * BATCH
    x_spec = _batch_rows(tq, D_MODEL) if batch_major_in else _rows(rows, D_MODEL)
    scratch = [
        pltpu.VMEM((sub_rows + CONV_HALO, D_CONV), f32),
        pltpu.VMEM((sub_rows // SSM_Q, SSM_COLS), f32),
        pltpu.VMEM((BATCH, SSM_COLS), f32),
    ]
    if batch_major_in:
        scratch.append(pltpu.VMEM((D_MODEL // LANES, rows, LANES), f32))
    return pl.pallas_call(
        functools.partial(_even_kernel, depth_idx=i, layer_idx=j),
        grid=(N_ROWS // rows,),
        in_specs=[x_spec] + [_layer_spec(a, j) for a in layer_params],
        out_specs=_rows(rows, D_MODEL),
        out_shape=jax.ShapeDtypeStruct((N_ROWS, D_MODEL), f32),
        scratch_shapes=scratch,
        compiler_params=_params(),
        name="even_mixer",
    )(x, *layer_params)


def _odd_kernel(x_ref, g_ref, win_ref, pw_ref, ps_ref, ng_ref, sw_ref, sb_ref, wout_ref,
                o_ref, z_ref, v_ref, m_ref, *, depth_idx, layer_idx):
    i = pl.program_id(0)
    rows = x_ref.shape[0]
    tq = rows // BATCH
    assert tq == CHUNK

    @pl.when(i == 0)
    def _():
        z_ref[0:POOL_HALO, :] = jnp.zeros((POOL_HALO, D_POOL), f32)

    x = x_ref[...]
    h = _rms(x, g_ref[depth_idx:depth_idx + 1, :]).astype(bf16)
    proj = _dot(h, win_ref[...])
    z = proj[:, 0:D_POOL]
    uv = jax.nn.gelu(proj[:, D_POOL:])
    su = uv[:, 0:D_SGU]
    sv = uv[:, D_SGU:]

    z_ref[POOL_HALO:POOL_HALO + rows, :] = z
    count = (i * tq + 1 + lax.broadcasted_iota(jnp.int32, (rows, 1), 0) // BATCH).astype(f32)
    yc = []
    for gi, win in enumerate(POOL_WINDOWS):
        cols = slice(gi * POOL_GROUP, (gi + 1) * POOL_GROUP)
        e = z_ref[:, cols]
        span = 1
        while span < win:
            sh = span * BATCH
            e = e[sh:, :] + e[:-sh, :]
            span *= 2
        wsum = e[e.shape[0] - rows:, :]
        pooled = wsum / jnp.minimum(count, float(win)) - z[:, cols]
        yc.append(_dot(pooled.astype(bf16), pw_ref[gi]))
    yc = jnp.concatenate(yc, axis=1) * ps_ref[layer_idx:layer_idx + 1, :]
    z_ref[0:POOL_HALO, :] = z_ref[rows:rows + POOL_HALO, :]

    v = _rms(sv, ng_ref[layer_idx:layer_idx + 1, :])
    tri = lax.broadcasted_iota(jnp.int32, (CHUNK, CHUNK), 0) >= lax.broadcasted_iota(
        jnp.int32, (CHUNK, CHUNK), 1)
    for hd in range(SGU_HEADS):
        v_ref[hd] = v[:, hd * SGU_HEAD_DIM:(hd + 1) * SGU_HEAD_DIM]
    for hd in range(SGU_HEADS):
        w_s = jnp.where(tri, sw_ref[hd], 0.0).astype(bf16)
        v_b = jnp.concatenate(
            [v_ref[hd, pl.ds(b, CHUNK, stride=BATCH), :].astype(bf16) for b in range(BATCH)],
            axis=1)
        mixed_b = _dot(w_s, v_b) + sb_ref[:, hd:hd + 1]
        for b in range(BATCH):
            m_ref[hd, pl.ds(b, CHUNK, stride=BATCH), :] = mixed_b[:, b * SGU_HEAD_DIM:(b + 1) * SGU_HEAD_DIM]
    yd = su * jnp.concatenate([m_ref[hd] for hd in range(SGU_HEADS)], axis=1)

    mix = _dot(yc.astype(bf16), wout_ref[0:D_POOL, :]) + _dot(yd.astype(bf16), wout_ref[D_POOL:, :])
    o_ref[...] = x + mix


def _odd_mixer(x, i, j, layer_params):
    rows = ODD_TQ * BATCH
    return pl.pallas_call(
        functools.partial(_odd_kernel, depth_idx=i, layer_idx=j),
        grid=(N_ROWS // rows,),
        in_specs=[_rows(rows, D_MODEL)] + [_layer_spec(a, j) for a in layer_params],
        out_specs=_rows(rows, D_MODEL),
        out_shape=jax.ShapeDtypeStruct((N_ROWS, D_MODEL), f32),
        scratch_shapes=[
            pltpu.VMEM((rows + POOL_HALO, D_POOL), f32),
            pltpu.VMEM((SGU_HEADS, rows, SGU_HEAD_DIM), f32),
            pltpu.VMEM((SGU_HEADS, rows, SGU_HEAD_DIM), f32),
        ],
        compiler_params=_params(),
        name="odd_mixer",
    )(x, *layer_params)


def _ffn_chunks():
    out, c0 = [], 0
    while c0 < D_FF:
        cw = min(256, D_FF - c0)
        out.append((c0, cw))
        c0 += cw
    return out


def _ffn_kernel(x_ref, g_ref, wup_ref, cw_ref, cb_ref, wdn_ref, gf_ref, o_ref, up_ref, act_ref,
                yt_ref=None, *, depth_idx):
    i = pl.program_id(0)
    rows = x_ref.shape[0]
    tq = rows // BATCH

    @pl.when(i == 0)
    def _():
        up_ref[...] = jnp.zeros_like(up_ref)

    x = x_ref[...]
    h = _rms(x, g_ref[depth_idx:depth_idx + 1, :]).astype(bf16)

    def conv(off, cw):
        cols = slice(off, off + cw)
        up = _dot(h, wup_ref[:, cols])
        ext = jnp.concatenate([up_ref[:, cols], up], axis=0)
        up_ref[:, cols] = up[rows - CONV_HALO:rows, :]
        y = ext[0:rows, :] * cw_ref[0:1, cols]
        for k in range(1, CONV_WIDTH):
            y = y + ext[k * BATCH:k * BATCH + rows, :] * cw_ref[k:k + 1, cols]
        return y + cb_ref[depth_idx:depth_idx + 1, cols]

    for c0, cw in _ffn_chunks():
        gate = conv(c0, cw)
        val = conv(D_FF + c0, cw)
        act_ref[:, c0:c0 + cw] = (jax.nn.silu(gate) * val).astype(bf16)

    out = x + _dot(act_ref[...], wdn_ref[...])
    if yt_ref is None:
        o_ref[...] = out
    else:
        out = _rms(out, gf_ref[...])
        for c in range(D_MODEL // LANES):
            yt_ref[c] = out[:, c * LANES:(c + 1) * LANES]
        for b in range(BATCH):
            for c in range(D_MODEL // LANES):
                o_ref[b, :, c * LANES:(c + 1) * LANES] = yt_ref[c, pl.ds(b, tq, stride=BATCH), :]


def _conv_ffn(x, j, layer_params, g_final, final):
    rows = FFN_TQ * BATCH
    scratch = [pltpu.VMEM((CONV_HALO, 2 * D_FF), f32), pltpu.VMEM((rows, D_FF), bf16)]
    if final:
        scratch.append(pltpu.VMEM((D_MODEL // LANES, rows, LANES), f32))
        out_spec = _batch_rows(FFN_TQ, D_MODEL)
        out_shape = jax.ShapeDtypeStruct((BATCH, SEQ, D_MODEL), f32)
    else:
        out_spec = _rows(rows, D_MODEL)
        out_shape = jax.ShapeDtypeStruct((N_ROWS, D_MODEL), f32)
    return pl.pallas_call(
        functools.partial(_ffn_kernel, depth_idx=j),
        grid=(N_ROWS // rows,),
        in_specs=([_rows(rows, D_MODEL)] + [_layer_spec(a, j) for a in layer_params]
                  + [_layer_spec(g_final, 0)]),
        out_specs=out_spec,
        out_shape=out_shape,
        scratch_shapes=scratch,
        compiler_params=_params(),
        name="conv_ffn",
    )(x, *layer_params, g_final)


def kernel(x, norm_mix_g, even_w_in, even_conv_w, ssm_log_step, ssm_a_re, ssm_a_im, ssm_b_re, ssm_b_im, ssm_c_re, ssm_c_im, ssm_d, ssm_glu_w, ssm_glu_b, even_w_out, odd_w_in, pool_w, pool_scale, sgu_norm_g, sgu_w, sgu_b, odd_w_out, norm_ffn_g, ffn_w_up, ffn_conv_w, ffn_conv_b, ffn_w_down, norm_final_g):
    assert x.shape == (BATCH, SEQ, D_MODEL)
    emat, qmat, tmat, l_re, l_im = _s5_matrices(
        ssm_log_step, ssm_a_re, ssm_a_im, ssm_b_re, ssm_b_im, ssm_c_re, ssm_c_im)
    even_params = (
        norm_mix_g, even_w_in.astype(bf16), even_conv_w,
        l_re, l_im, emat, qmat, tmat, ssm_d,
        ssm_glu_w.astype(bf16), ssm_glu_b, even_w_out.astype(bf16))

    odd_params = (
        norm_mix_g, odd_w_in.astype(bf16), pool_w.astype(bf16),
        pool_scale, sgu_norm_g, sgu_w, jnp.swapaxes(sgu_b, 1, 2),
        odd_w_out.astype(bf16))

    ffn_params = (
        norm_ffn_g, ffn_w_up.astype(bf16), ffn_conv_w, ffn_conv_b,
        ffn_w_down.astype(bf16))
    g_final = norm_final_g.reshape(1, D_MODEL)

    xt = x
    for i in range(DEPTH):
        j = i // 2
        if i % 2 == 0:
            xt = _even_mixer(xt, i, j, even_params, batch_major_in=(i == 0))
        else:
            xt = _odd_mixer(xt, i, j, odd_params)
        xt = _conv_ffn(xt, i, ffn_params, g_final, final=(i == DEPTH - 1))
    return xt
```

```python
import functools

import jax
import jax.numpy as jnp
from jax import lax
from jax.experimental import pallas as pl
from jax.experimental.pallas import tpu as pltpu

D_MODEL = 1024
BATCH = 8
SEQ = 4096
DEPTH = 4
D_CONV = D_MODEL // 2
CONV_WIDTH = 3
D_SSM = D_MODEL // 2
SSM_GROUP = 16
N_SSM_GROUPS = D_SSM // SSM_GROUP
SSM_STATE = 64
D_POOL = D_MODEL // 2
POOL_WINDOWS = (2, 4, 8, 16)
POOL_GROUP = D_POOL // len(POOL_WINDOWS)
D_SGU = D_MODEL // 2
SGU_HEADS = 4
SGU_HEAD_DIM = D_SGU // SGU_HEADS
CHUNK = 128
D_FF = ((8 * D_MODEL // 3 + 127) // 128) * 128
EPS = 1e-6

N_ROWS = SEQ * BATCH
LANES = 128
V7X_VMEM_BYTES = 64 * 1024 * 1024
VMEM_LIMIT_BYTES = V7X_VMEM_BYTES - 8 * 1024 * 1024

CONV_HALO = (CONV_WIDTH - 1) * BATCH
POOL_HALO = max(POOL_WINDOWS) * BATCH
SSM_Q = 4
SLAB_GROUPS = LANES // SSM_GROUP
N_SLABS = N_SSM_GROUPS // SLAB_GROUPS
SLAB_STATES = SLAB_GROUPS * SSM_STATE
SSM_COLS = 2 * N_SLABS * SLAB_STATES

EVEN_TQ = 128
EVEN_TQ_BATCH_MAJOR = 128
EVEN_SUB_TQ = 64
FFN_TQ = 128
FFN_CAST_BLOCKS = 16
ODD_TQ = CHUNK

f32 = jnp.float32
bf16 = jnp.bfloat16


def _rms(x, g):
    return x * lax.rsqrt(jnp.mean(x * x, axis=-1, keepdims=True) + EPS) * g


def _dot(a, b):
    return jnp.dot(a, b, preferred_element_type=f32)


def _layer_spec(arr, j):
    nd = arr.ndim
    if nd == 2:
        return pl.BlockSpec(arr.shape, lambda i: (0, 0))
    return pl.BlockSpec((None,) + arr.shape[1:], lambda i: (j,) + (0,) * (nd - 1),
                        pipeline_mode=pl.Buffered(1))


def _rows(r, d):
    return pl.BlockSpec((r, d), lambda i: (i, 0))


def _batch_rows(tq, d):
    return pl.BlockSpec((BATCH, tq, d), lambda i: (0, i, 0))


def _ffn_weight_cast_specs(n_steps, layer):
    up_rows, dn_rows = D_MODEL // FFN_CAST_BLOCKS, D_FF // FFN_CAST_BLOCKS
    blk = lambda i: i * FFN_CAST_BLOCKS // n_steps
    in_specs = [pl.BlockSpec((None, up_rows, 2 * D_FF), lambda i: (layer, blk(i), 0)),
                pl.BlockSpec((None, dn_rows, D_MODEL), lambda i: (layer, blk(i), 0))]
    out_specs = [pl.BlockSpec((up_rows, 2 * D_FF), lambda i: (blk(i), 0)),
                 pl.BlockSpec((dn_rows, D_MODEL), lambda i: (blk(i), 0))]
    out_shapes = [jax.ShapeDtypeStruct((D_MODEL, 2 * D_FF), bf16),
                  jax.ShapeDtypeStruct((D_FF, D_MODEL), bf16)]
    return in_specs, out_specs, out_shapes


def _params():
    return pltpu.CompilerParams(dimension_semantics=("arbitrary",),
                                vmem_limit_bytes=VMEM_LIMIT_BYTES)


def _lam_power(m, a_re, a_im, step):
    mag = jnp.exp((m * a_re) * step)
    ang = (m * a_im) * step
    return mag * jnp.cos(ang), mag * jnp.sin(ang)


def _s5_matrix_kernel(lsr_ref, arr_ref, air_ref, btre_ref, btim_ref, cre_ref, cim_ref,
                      lsc_ref, arc_ref, aic_ref, e_ref, q_ref, t_ref, lre_ref, lim_ref):
    rows, p = SLAB_GROUPS * SSM_GROUP, SSM_STATE
    step = jnp.exp(lsr_ref[0])
    a_re = arr_ref[0]
    a_im = air_ref[0]
    power = lambda m: _lam_power(m, a_re, a_im, step)

    l_re, l_im = power(1)
    n_re = l_re - 1.0
    den = a_re * a_re + a_im * a_im
    k_re = (n_re * a_re + l_im * a_im) / den
    k_im = (l_im * a_re - n_re * a_im) / den
    bt_re = btre_ref[0]
    bt_im = btim_ref[0]
    bb_re = k_re * bt_re - k_im * bt_im
    bb_im = k_re * bt_im + k_im * bt_re
    c_re = cre_ref[0]
    c_im = cim_ref[0]

    col = lax.broadcasted_iota(jnp.int32, (p, SLAB_STATES), 1)
    spread = (lax.bitwise_and(col, p - 1) == lax.broadcasted_iota(jnp.int32, (p, SLAB_STATES), 0)
              ).astype(bf16)
    same_group = (
        lax.shift_right_logical(lax.broadcasted_iota(jnp.int32, (rows, SLAB_STATES), 0),
                                SSM_GROUP.bit_length() - 1)
        == lax.shift_right_logical(lax.broadcasted_iota(jnp.int32, (rows, SLAB_STATES), 1),
                                   SSM_STATE.bit_length() - 1))

    def tile(v16):
        return jnp.where(same_group, _dot(v16, spread), 0.0)

    def split(v):
        hi = v.astype(bf16)
        return hi, (v - hi.astype(f32)).astype(bf16)

    def tile_split(v):
        hi, lo = split(v)
        return tile(hi), tile(lo)

    def dot_nt(a, b):
        return lax.dot_general(a.astype(bf16), b.astype(bf16), (((1,), (1,)), ((), ())),
                               preferred_element_type=f32)

    def dot_nt3(a, b):
        return dot_nt(a[0], b[0]) + dot_nt(a[0], b[1]) + dot_nt(a[1], b[0])

    bb_re_t, bb_im_t = tile_split(bb_re), tile_split(bb_im)
    t_blocks = []
    for m in range(SSM_Q + 1):
        p_re, p_im = power(m)
        cp_re_t = tile_split(c_re * p_re - c_im * p_im)
        cp_im_t = tile_split(c_re * p_im + c_im * p_re)
        if m < SSM_Q:
            i = SSM_Q - 1 - m
            e_ref[0, i * rows:(i + 1) * rows, 0:SLAB_STATES] = (
                tile((p_re * bb_re - p_im * bb_im).astype(bf16)).astype(bf16))
            e_ref[0, i * rows:(i + 1) * rows, SLAB_STATES:] = (
                tile((p_re * bb_im + p_im * bb_re).astype(bf16)).astype(bf16))
            t_blocks.append(dot_nt3(bb_re_t, cp_re_t) - dot_nt3(bb_im_t, cp_im_t))
        if m >= 1:
            j = m - 1
            q_ref[0, 0:SLAB_STATES, j * rows:(j + 1) * rows] = cp_re_t[0].T.astype(bf16)
            q_ref[0, SLAB_STATES:, j * rows:(j + 1) * rows] = (-cp_im_t[0]).T.astype(bf16)
    for i in range(SSM_Q):
        for j in range(SSM_Q):
            blk = t_blocks[j - i] if j >= i else jnp.zeros((rows, rows), f32)
            t_ref[0, i * rows:(i + 1) * rows, j * rows:(j + 1) * rows] = blk.astype(bf16)

    lq_re, lq_im = _lam_power(SSM_Q, arc_ref[0], aic_ref[0], jnp.exp(lsc_ref[0]))
    lre_ref[0] = lq_re
    lim_ref[0] = lq_im


def _s5_matrices(log_step, a_re, a_im, b_re, b_im, c_re, c_im):
    nl = log_step.shape[0]
    n = nl * N_SLABS
    rows, p, hg = SLAB_GROUPS * SSM_GROUP, SSM_STATE, SSM_GROUP
    per_row = lambda v: jnp.repeat(v, hg, axis=1).reshape(n, rows, -1)
    by_row = lambda v: v.reshape(n, rows, p)
    compact = lambda v: v.reshape(n, SLAB_GROUPS, -1)

    def spec(*dims):
        return pl.BlockSpec((1,) + dims, lambda i: (i,) + (0,) * len(dims))

    wide = 2 * SLAB_STATES
    emat, qmat, tmat, l_re, l_im = pl.pallas_call(
        _s5_matrix_kernel,
        grid=(n,),
        in_specs=[spec(rows, 1), spec(rows, p), spec(rows, p), spec(rows, p), spec(rows, p),
                  spec(rows, p), spec(rows, p), spec(SLAB_GROUPS, 1), spec(SLAB_GROUPS, p),
                  spec(SLAB_GROUPS, p)],
        out_specs=(spec(SSM_Q * rows, wide), spec(wide, SSM_Q * rows), spec(SSM_Q * rows, SSM_Q * rows),
                   spec(SLAB_GROUPS, p), spec(SLAB_GROUPS, p)),
        out_shape=(jax.ShapeDtypeStruct((n, SSM_Q * rows, wide), bf16),
                   jax.ShapeDtypeStruct((n, wide, SSM_Q * rows), bf16),
                   jax.ShapeDtypeStruct((n, SSM_Q * rows, SSM_Q * rows), bf16),
                   jax.ShapeDtypeStruct((n, SLAB_GROUPS, p), f32),
                   jax.ShapeDtypeStruct((n, SLAB_GROUPS, p), f32)),
        compiler_params=_params(),
        name="s5_matrices",
    )(per_row(log_step[:, :, None]), per_row(a_re), per_row(a_im),
      by_row(jnp.swapaxes(b_re, 2, 3)), by_row(jnp.swapaxes(b_im, 2, 3)), by_row(c_re), by_row(c_im),
      compact(log_step), compact(a_re), compact(a_im))
    split = lambda m: m.reshape((nl, N_SLABS) + m.shape[1:])
    lam = lambda v: v.reshape(nl, 1, N_SSM_GROUPS * SSM_STATE)
    return split(emat), split(qmat), split(tmat), lam(l_re), lam(l_im)


def _even_kernel(x_ref, g_ref, win_ref, cw_ref, lr_ref, li_ref, em_ref, qm_ref, tm_ref,
                 dsk_ref, gw_ref, gb_ref, wout_ref, fup_ref, fdn_ref,
                 o_ref, fup16_ref, fdn16_ref, cx_ref, bu_ref, st_ref,
                 xt_ref=None, *, depth_idx, layer_idx):
    i = pl.program_id(0)
    fup16_ref[...] = fup_ref[...].astype(bf16)
    fdn16_ref[...] = fdn_ref[...].astype(bf16)
    gain = g_ref[depth_idx:depth_idx + 1, :]
    d_skip = dsk_ref[layer_idx:layer_idx + 1, :]
    glu_b = gb_ref[layer_idx:layer_idx + 1, :]
    tile_rows = o_ref.shape[0]

    @pl.when(i == 0)
    def _():
        cx_ref[0:CONV_HALO, :] = jnp.zeros((CONV_HALO, D_CONV), f32)
        st_ref[...] = jnp.zeros_like(st_ref)

    if xt_ref is not None:
        for b in range(BATCH):
            for c in range(D_MODEL // LANES):
                xt_ref[c, pl.ds(b, tile_rows // BATCH, stride=BATCH), :] = (
                    x_ref[b, :, c * LANES:(c + 1) * LANES])

    rows = EVEN_SUB_TQ * BATCH
    tq = EVEN_SUB_TQ
    n_sub = tile_rows // rows
    assert D_CONV == D_SSM and N_SLABS == 4

    def normed(sub):
        r_lo = sub * rows
        if xt_ref is None:
            x = x_ref[r_lo:r_lo + rows, :]
        else:
            x = jnp.concatenate(
                [xt_ref[c, r_lo:r_lo + rows, :] for c in range(D_MODEL // LANES)], axis=1)
        return x, _rms(x, gain).astype(bf16)

    def in_slice(h, q):
        return _dot(h, win_ref[:, q * D_CONV:(q + 1) * D_CONV])

    x, h = normed(0)
    xa, ba, ca, u = [in_slice(h, q) for q in range(4)]
    ya = _even_conv(xa, ba, ca, cw_ref, cx_ref)
    u4 = _fold_time(u)
    for s in range(N_SLABS):
        _s5_expand(u4[s], s, em_ref, bu_ref)

    for sub in range(n_sub):
        last = sub + 1 == n_sub
        if not last:
            x_n, h_n = normed(sub + 1)
            parts_n = []
        y4 = []
        for s in range(N_SLABS):
            _s5_scan(s, tq // SSM_Q, lr_ref, li_ref, bu_ref, st_ref)
            if not last:
                parts_n.append(in_slice(h_n, s))
            y4.append(_s5_readout(s, u4[s], qm_ref, tm_ref, bu_ref))
        y = jax.nn.gelu(_unfold_time(y4) + d_skip * u)
        if not last:
            ya_n = _even_conv(parts_n[0], parts_n[1], parts_n[2], cw_ref, cx_ref)
            u_n = parts_n[3]
            u4_n = _fold_time(u_n)
            for s in range(N_SLABS // 2):
                _s5_expand(u4_n[s], s, em_ref, bu_ref)
        gate = _dot(y.astype(bf16), gw_ref[...]) + glu_b
        if not last:
            for s in range(N_SLABS // 2, N_SLABS):
                _s5_expand(u4_n[s], s, em_ref, bu_ref)
        yb = y * jax.nn.sigmoid(gate)
        mix = _dot(ya.astype(bf16), wout_ref[0:D_CONV, :]) + _dot(yb.astype(bf16), wout_ref[D_CONV:, :])
        o_ref[sub * rows:(sub + 1) * rows, :] = x + mix
        if not last:
            x, ya, u, u4 = x_n, ya_n, u_n, u4_n


def _even_conv(xa, ba, ca, cw_ref, cx_ref):
    rows = xa.shape[0]
    cx_ref[CONV_HALO:CONV_HALO + rows, :] = ca * xa
    conv = cx_ref[0:rows, :] * cw_ref[0:1, :]
    for k in range(1, CONV_WIDTH):
        conv = conv + cx_ref[k * BATCH:k * BATCH + rows, :] * cw_ref[k:k + 1, :]
    ya = ba * conv
    cx_ref[0:CONV_HALO, :] = cx_ref[rows:rows + CONV_HALO, :]
    return ya


def _fold_time(u):
    n_chunks = u.shape[0] // (SSM_Q * BATCH)
    at_offset = [
        jnp.concatenate([u[(SSM_Q * k + i) * BATCH:(SSM_Q * k + i + 1) * BATCH, :]
                         for k in range(n_chunks)], axis=0)
        for i in range(SSM_Q)]
    return [jnp.concatenate([a[:, s * LANES:(s + 1) * LANES] for a in at_offset], axis=1).astype(bf16)
            for s in range(N_SLABS)]


def _unfold_time(y4):
    n_chunks = y4[0].shape[0] // BATCH
    at_offset = [jnp.concatenate([y[:, j * LANES:(j + 1) * LANES] for y in y4], axis=1)
                 for j in range(SSM_Q)]
    return jnp.concatenate([at_offset[j][k * BATCH:(k + 1) * BATCH, :]
                            for k in range(n_chunks) for j in range(SSM_Q)], axis=0)


def _s5_expand(u4s, s, em_ref, bu_ref):
    c0 = 2 * SLAB_STATES * s
    bu_ref[:, c0:c0 + 2 * SLAB_STATES] = _dot(u4s, em_ref[s])


def _s5_scan(s, n_chunks, lr_ref, li_ref, bu_ref, st_ref):
    c_re = 2 * SLAB_STATES * s
    c_im = c_re + SLAB_STATES
    re = slice(c_re, c_re + SLAB_STATES)
    im = slice(c_im, c_im + SLAB_STATES)
    lam = slice(SLAB_STATES * s, SLAB_STATES * (s + 1))
    ar = jnp.broadcast_to(lr_ref[:, lam], (BATCH, SLAB_STATES))
    ai = jnp.broadcast_to(li_ref[:, lam], (BATCH, SLAB_STATES))
    s_re = st_ref[:, re]
    s_im = st_ref[:, im]
    for k in range(n_chunks):
        r = slice(k * BATCH, (k + 1) * BATCH)
        b_re = bu_ref[r, re]
        b_im = bu_ref[r, im]
        bu_ref[r, re] = s_re
        bu_ref[r, im] = s_im
        s_re, s_im = ar * s_re - ai * s_im + b_re, ar * s_im + ai * s_re + b_im
    st_ref[:, re] = s_re
    st_ref[:, im] = s_im


def _s5_readout(s, u4s, qm_ref, tm_ref, bu_ref):
    c_re = 2 * SLAB_STATES * s
    c_im = c_re + SLAB_STATES
    return (_dot(bu_ref[:, c_re:c_re + SLAB_STATES].astype(bf16), qm_ref[s, 0:SLAB_STATES, :])
            + _dot(bu_ref[:, c_im:c_im + SLAB_STATES].astype(bf16), qm_ref[s, SLAB_STATES:, :])
            + _dot(u4s, tm_ref[s]))


def _even_mixer(x, i, j, layer_params, ffn_w_up, ffn_w_down, batch_major_in):
    tq = EVEN_TQ_BATCH_MAJOR if batch_major_in else EVEN_TQ
    rows = tq * BATCH
    cast_in, cast_out, cast_shapes = _ffn_weight_cast_specs(N_ROWS // rows, i)
    sub_rows = EVEN_SUB_TQ * BATCH
    x_spec = _batch_rows(tq, D_MODEL) if batch_major_in else _rows(rows, D_MODEL)
    scratch = [
        pltpu.VMEM((sub_rows + CONV_HALO, D_CONV), f32),
        pltpu.VMEM((sub_rows // SSM_Q, SSM_COLS), f32),
        pltpu.VMEM((BATCH, SSM_COLS), f32),
    ]
    if batch_major_in:
        scratch.append(pltpu.VMEM((D_MODEL // LANES, rows, LANES), f32))
    return pl.pallas_call(
        functools.partial(_even_kernel, depth_idx=i, layer_idx=j),
        grid=(N_ROWS // rows,),
        in_specs=[x_spec] + [_layer_spec(a, j) for a in layer_params] + cast_in,
        out_specs=[_rows(rows, D_MODEL)] + cast_out,
        out_shape=[jax.ShapeDtypeStruct((N_ROWS, D_MODEL), f32)] + cast_shapes,
        scratch_shapes=scratch,
        compiler_params=_params(),
        name="even_mixer",
    )(x, *layer_params, ffn_w_up, ffn_w_down)


def _odd_kernel(x_ref, g_ref, win_ref, pw_ref, ps_ref, ng_ref, sw_ref, sb_ref, wout_ref,
                fup_ref, fdn_ref, o_ref, fup16_ref, fdn16_ref, z_ref, v_ref, m_ref,
                *, depth_idx, layer_idx):
    i = pl.program_id(0)
    fup16_ref[...] = fup_ref[...].astype(bf16)
    fdn16_ref[...] = fdn_ref[...].astype(bf16)
    tq = CHUNK
    rows = tq * BATCH
    n_chunks = x_ref.shape[0] // rows

    @pl.when(i == 0)
    def _():
        z_ref[0:POOL_HALO, :] = jnp.zeros((POOL_HALO, D_POOL), f32)

    def front(c):
        x = x_ref[c * rows:(c + 1) * rows, :]
        h = _rms(x, g_ref[depth_idx:depth_idx + 1, :]).astype(bf16)
        proj = _dot(h, win_ref[...])
        uv = jax.nn.gelu(proj[:, D_POOL:])
        return x, proj[:, 0:D_POOL], uv[:, 0:D_SGU], uv[:, D_SGU:]

    cur = front(0)
    for c in range(n_chunks):
        nxt = front(c + 1) if c + 1 < n_chunks else None
        x, z, su, sv = cur
        mix = _odd_mix(i * n_chunks + c, z, su, sv, pw_ref, ps_ref, ng_ref, sw_ref, sb_ref, wout_ref,
                       z_ref, v_ref, m_ref, layer_idx)
        o_ref[c * rows:(c + 1) * rows, :] = x + mix
        cur = nxt


def _odd_mix(chunk_idx, z, su, sv, pw_ref, ps_ref, ng_ref, sw_ref, sb_ref, wout_ref,
             z_ref, v_ref, m_ref, layer_idx):
    rows = z.shape[0]
    tq = rows // BATCH

    z_ref[POOL_HALO:POOL_HALO + rows, :] = z
    count = (chunk_idx * tq + 1 + lax.broadcasted_iota(jnp.int32, (rows, 1), 0) // BATCH).astype(f32)
    yc = []
    for gi, win in enumerate(POOL_WINDOWS):
        cols = slice(gi * POOL_GROUP, (gi + 1) * POOL_GROUP)
        e = z_ref[:, cols]
        span = 1
        while span < win:
            sh = span * BATCH
            e = e[sh:, :] + e[:-sh, :]
            span *= 2
        wsum = e[e.shape[0] - rows:, :]
        pooled = wsum / jnp.minimum(count, float(win)) - z[:, cols]
        yc.append(_dot(pooled.astype(bf16), pw_ref[gi]))
    yc = jnp.concatenate(yc, axis=1) * ps_ref[layer_idx:layer_idx + 1, :]
    z_ref[0:POOL_HALO, :] = z_ref[rows:rows + POOL_HALO, :]

    v = _rms(sv, ng_ref[layer_idx:layer_idx + 1, :])
    tri = lax.broadcasted_iota(jnp.int32, (CHUNK, CHUNK), 0) >= lax.broadcasted_iota(
        jnp.int32, (CHUNK, CHUNK), 1)
    for hd in range(SGU_HEADS):
        v_ref[hd] = v[:, hd * SGU_HEAD_DIM:(hd + 1) * SGU_HEAD_DIM]
    for hd in range(SGU_HEADS):
        w_s = jnp.where(tri, sw_ref[hd], 0.0).astype(bf16)
        v_b = jnp.concatenate(
            [v_ref[hd, pl.ds(b, CHUNK, stride=BATCH), :].astype(bf16) for b in range(BATCH)],
            axis=1)
        mixed_b = _dot(w_s, v_b) + sb_ref[:, hd:hd + 1]
        for b in range(BATCH):
            m_ref[hd, pl.ds(b, CHUNK, stride=BATCH), :] = mixed_b[:, b * SGU_HEAD_DIM:(b + 1) * SGU_HEAD_DIM]
    yd = su * jnp.concatenate([m_ref[hd] for hd in range(SGU_HEADS)], axis=1)

    return _dot(yc.astype(bf16), wout_ref[0:D_POOL, :]) + _dot(yd.astype(bf16), wout_ref[D_POOL:, :])


def _odd_mixer(x, i, j, layer_params, ffn_w_up, ffn_w_down):
    rows = ODD_TQ * BATCH
    chunk_rows = CHUNK * BATCH
    cast_in, cast_out, cast_shapes = _ffn_weight_cast_specs(N_ROWS // rows, i)
    return pl.pallas_call(
        functools.partial(_odd_kernel, depth_idx=i, layer_idx=j),
        grid=(N_ROWS // rows,),
        in_specs=[_rows(rows, D_MODEL)] + [_layer_spec(a, j) for a in layer_params] + cast_in,
        out_specs=[_rows(rows, D_MODEL)] + cast_out,
        out_shape=[jax.ShapeDtypeStruct((N_ROWS, D_MODEL), f32)] + cast_shapes,
        scratch_shapes=[
            pltpu.VMEM((chunk_rows + POOL_HALO, D_POOL), f32),
            pltpu.VMEM((SGU_HEADS, chunk_rows, SGU_HEAD_DIM), f32),
            pltpu.VMEM((SGU_HEADS, chunk_rows, SGU_HEAD_DIM), f32),
        ],
        compiler_params=_params(),
        name="odd_mixer",
    )(x, *layer_params, ffn_w_up, ffn_w_down)


def _ffn_chunks():
    out, c0 = [], 0
    while c0 < D_FF:
        cw = min(256, D_FF - c0)
        out.append((c0, cw))
        c0 += cw
    return out


def _ffn_kernel(x_ref, g_ref, wup_ref, cw_ref, cb_ref, wdn_ref, gf_ref, o_ref, up_ref, act_ref,
                yt_ref=None, *, depth_idx):
    i = pl.program_id(0)
    rows = x_ref.shape[0]
    tq = rows // BATCH

    @pl.when(i == 0)
    def _():
        up_ref[...] = jnp.zeros_like(up_ref)

    x = x_ref[...]
    h = _rms(x, g_ref[depth_idx:depth_idx + 1, :]).astype(bf16)

    def conv(off, cw):
        cols = slice(off, off + cw)
        up = _dot(h, wup_ref[:, cols])
        ext = jnp.concatenate([up_ref[:, cols], up], axis=0)
        up_ref[:, cols] = up[rows - CONV_HALO:rows, :]
        y = ext[0:rows, :] * cw_ref[0:1, cols]
        for k in range(1, CONV_WIDTH):
            y = y + ext[k * BATCH:k * BATCH + rows, :] * cw_ref[k:k + 1, cols]
        return y + cb_ref[depth_idx:depth_idx + 1, cols]

    for c0, cw in _ffn_chunks():
        gate = conv(c0, cw)
        val = conv(D_FF + c0, cw)
        act_ref[:, c0:c0 + cw] = (jax.nn.silu(gate) * val).astype(bf16)

    out = x + _dot(act_ref[...], wdn_ref[...])
    if yt_ref is None:
        o_ref[...] = out
    else:
        out = _rms(out, gf_ref[...])
        for c in range(D_MODEL // LANES):
            yt_ref[c] = out[:, c * LANES:(c + 1) * LANES]
        for b in range(BATCH):
            for c in range(D_MODEL // LANES):
                o_ref[b, :, c * LANES:(c + 1) * LANES] = yt_ref[c, pl.ds(b, tq, stride=BATCH), :]


def _conv_ffn(x, j, layer_params, g_final, final):
    rows = FFN_TQ * BATCH
    scratch = [pltpu.VMEM((CONV_HALO, 2 * D_FF), f32), pltpu.VMEM((rows, D_FF), bf16)]
    if final:
        scratch.append(pltpu.VMEM((D_MODEL // LANES, rows, LANES), f32))
        out_spec = _batch_rows(FFN_TQ, D_MODEL)
        out_shape = jax.ShapeDtypeStruct((BATCH, SEQ, D_MODEL), f32)
    else:
        out_spec = _rows(rows, D_MODEL)
        out_shape = jax.ShapeDtypeStruct((N_ROWS, D_MODEL), f32)
    return pl.pallas_call(
        functools.partial(_ffn_kernel, depth_idx=j),
        grid=(N_ROWS // rows,),
        in_specs=([_rows(rows, D_MODEL)] + [_layer_spec(a, j) for a in layer_params]
                  + [_layer_spec(g_final, 0)]),
        out_specs=out_spec,
        out_shape=out_shape,
        scratch_shapes=scratch,
        compiler_params=_params(),
        name="conv_ffn",
    )(x, *layer_params, g_final)


def kernel(x, norm_mix_g, even_w_in, even_conv_w, ssm_log_step, ssm_a_re, ssm_a_im, ssm_b_re, ssm_b_im, ssm_c_re, ssm_c_im, ssm_d, ssm_glu_w, ssm_glu_b, even_w_out, odd_w_in, pool_w, pool_scale, sgu_norm_g, sgu_w, sgu_b, odd_w_out, norm_ffn_g, ffn_w_up, ffn_conv_w, ffn_conv_b, ffn_w_down, norm_final_g):
    assert x.shape == (BATCH, SEQ, D_MODEL)
    emat, qmat, tmat, l_re, l_im = _s5_matrices(
        ssm_log_step, ssm_a_re, ssm_a_im, ssm_b_re, ssm_b_im, ssm_c_re, ssm_c_im)
    even_params = (
        norm_mix_g, even_w_in.astype(bf16), even_conv_w,
        l_re, l_im, emat, qmat, tmat, ssm_d,
        ssm_glu_w.astype(bf16), ssm_glu_b, even_w_out.astype(bf16))

    odd_params = (
        norm_mix_g, odd_w_in.astype(bf16), pool_w.astype(bf16),
        pool_scale, sgu_norm_g, sgu_w, jnp.swapaxes(sgu_b, 1, 2),
        odd_w_out.astype(bf16))

    g_final = norm_final_g.reshape(1, D_MODEL)

    xt = x
    for i in range(DEPTH):
        j = i // 2
        if i % 2 == 0:
            xt, w_up, w_down = _even_mixer(xt, i, j, even_params, ffn_w_up, ffn_w_down,
                                           batch_major_in=(i == 0))
        else:
            xt, w_up, w_down = _odd_mixer(xt, i, j, odd_params, ffn_w_up, ffn_w_down)
        ffn_params = (norm_ffn_g, w_up, ffn_conv_w, ffn_conv_b, w_down)
        xt = _conv_ffn(xt, i, ffn_params, g_final, final=(i == DEPTH - 1))
    return xt
```

```python
import functools

import jax
import jax.numpy as jnp
from jax import lax
from jax.experimental import pallas as pl
from jax.experimental.pallas import tpu as pltpu

D_MODEL = 1024
BATCH = 8
SEQ = 4096
DEPTH = 4
D_CONV = D_MODEL // 2
CONV_WIDTH = 3
D_SSM = D_MODEL // 2
SSM_GROUP = 16
N_SSM_GROUPS = D_SSM // SSM_GROUP
SSM_STATE = 64
D_POOL = D_MODEL // 2
POOL_WINDOWS = (2, 4, 8, 16)
POOL_GROUP = D_POOL // len(POOL_WINDOWS)
D_SGU = D_MODEL // 2
SGU_HEADS = 4
SGU_HEAD_DIM = D_SGU // SGU_HEADS
CHUNK = 128
D_FF = ((8 * D_MODEL // 3 + 127) // 128) * 128
EPS = 1e-6

N_ROWS = SEQ * BATCH
LANES = 128
V7X_VMEM_BYTES = 64 * 1024 * 1024
VMEM_LIMIT_BYTES = V7X_VMEM_BYTES - 8 * 1024 * 1024

CONV_HALO = (CONV_WIDTH - 1) * BATCH
POOL_HALO = max(POOL_WINDOWS) * BATCH
SSM_Q = 4
SLAB_GROUPS = LANES // SSM_GROUP
N_SLABS = N_SSM_GROUPS // SLAB_GROUPS
SLAB_STATES = SLAB_GROUPS * SSM_STATE
SSM_COLS = 2 * N_SLABS * SLAB_STATES

EVEN_TQ = 128
EVEN_TQ_BATCH_MAJOR = 128
EVEN_SUB_TQ = 64
FFN_TQ = 128
FFN_CAST_BLOCKS = 16
ODD_TQ = CHUNK

f32 = jnp.float32
bf16 = jnp.bfloat16


def _rms(x, g):
    return x * lax.rsqrt(jnp.mean(x * x, axis=-1, keepdims=True) + EPS) * g


def _dot(a, b):
    return jnp.dot(a, b, preferred_element_type=f32)


def _layer_spec(arr, j):
    nd = arr.ndim
    if nd == 2:
        return pl.BlockSpec(arr.shape, lambda i: (0, 0))
    return pl.BlockSpec((None,) + arr.shape[1:], lambda i: (j,) + (0,) * (nd - 1),
                        pipeline_mode=pl.Buffered(1))


def _rows(r, d):
    return pl.BlockSpec((r, d), lambda i: (i, 0))


def _batch_rows(tq, d):
    return pl.BlockSpec((BATCH, tq, d), lambda i: (0, i, 0))


def _ffn_weight_cast_specs(n_steps, layer):
    up_rows, dn_rows = D_MODEL // FFN_CAST_BLOCKS, D_FF // FFN_CAST_BLOCKS
    blk = lambda i: i * FFN_CAST_BLOCKS // n_steps
    in_specs = [pl.BlockSpec((None, up_rows, 2 * D_FF), lambda i: (layer, blk(i), 0)),
                pl.BlockSpec((None, dn_rows, D_MODEL), lambda i: (layer, blk(i), 0))]
    out_specs = [pl.BlockSpec((up_rows, 2 * D_FF), lambda i: (blk(i), 0)),
                 pl.BlockSpec((dn_rows, D_MODEL), lambda i: (blk(i), 0))]
    out_shapes = [jax.ShapeDtypeStruct((D_MODEL, 2 * D_FF), bf16),
                  jax.ShapeDtypeStruct((D_FF, D_MODEL), bf16)]
    return in_specs, out_specs, out_shapes


def _params():
    return pltpu.CompilerParams(dimension_semantics=("arbitrary",),
                                vmem_limit_bytes=VMEM_LIMIT_BYTES)


def _lam_power(m, a_re, a_im, step):
    mag = jnp.exp((m * a_re) * step)
    ang = (m * a_im) * step
    return mag * jnp.cos(ang), mag * jnp.sin(ang)


def _s5_matrix_kernel(lsr_ref, arr_ref, air_ref, btre_ref, btim_ref, cre_ref, cim_ref,
                      lsc_ref, arc_ref, aic_ref, *refs, n_cast):
    e_ref, q_ref, t_ref, lre_ref, lim_ref = refs[n_cast:n_cast + 5]
    for src, dst in zip(refs[:n_cast], refs[n_cast + 5:]):
        dst[...] = src[...].astype(bf16)

    rows, p = SLAB_GROUPS * SSM_GROUP, SSM_STATE
    step = jnp.exp(lsr_ref[0])
    a_re = arr_ref[0]
    a_im = air_ref[0]
    power = lambda m: _lam_power(m, a_re, a_im, step)

    l_re, l_im = power(1)
    n_re = l_re - 1.0
    den = a_re * a_re + a_im * a_im
    k_re = (n_re * a_re + l_im * a_im) / den
    k_im = (l_im * a_re - n_re * a_im) / den
    bt_re = btre_ref[0]
    bt_im = btim_ref[0]
    bb_re = k_re * bt_re - k_im * bt_im
    bb_im = k_re * bt_im + k_im * bt_re
    c_re = cre_ref[0]
    c_im = cim_ref[0]

    col = lax.broadcasted_iota(jnp.int32, (p, SLAB_STATES), 1)
    spread = (lax.bitwise_and(col, p - 1) == lax.broadcasted_iota(jnp.int32, (p, SLAB_STATES), 0)
              ).astype(bf16)
    same_group = (
        lax.shift_right_logical(lax.broadcasted_iota(jnp.int32, (rows, SLAB_STATES), 0),
                                SSM_GROUP.bit_length() - 1)
        == lax.shift_right_logical(lax.broadcasted_iota(jnp.int32, (rows, SLAB_STATES), 1),
                                   SSM_STATE.bit_length() - 1))

    def tile(v16):
        return jnp.where(same_group, _dot(v16, spread), 0.0)

    def split(v):
        hi = v.astype(bf16)
        return hi, (v - hi.astype(f32)).astype(bf16)

    def tile_split(v):
        hi, lo = split(v)
        return tile(hi), tile(lo)

    def dot_nt(a, b):
        return lax.dot_general(a.astype(bf16), b.astype(bf16), (((1,), (1,)), ((), ())),
                               preferred_element_type=f32)

    def dot_nt3(a, b):
        return dot_nt(a[0], b[0]) + dot_nt(a[0], b[1]) + dot_nt(a[1], b[0])

    bb_re_t, bb_im_t = tile_split(bb_re), tile_split(bb_im)
    t_blocks = []
    for m in range(SSM_Q + 1):
        p_re, p_im = power(m)
        cp_re_t = tile_split(c_re * p_re - c_im * p_im)
        cp_im_t = tile_split(c_re * p_im + c_im * p_re)
        if m < SSM_Q:
            i = SSM_Q - 1 - m
            e_ref[0, i * rows:(i + 1) * rows, 0:SLAB_STATES] = (
                tile((p_re * bb_re - p_im * bb_im).astype(bf16)).astype(bf16))
            e_ref[0, i * rows:(i + 1) * rows, SLAB_STATES:] = (
                tile((p_re * bb_im + p_im * bb_re).astype(bf16)).astype(bf16))
            t_blocks.append(dot_nt3(bb_re_t, cp_re_t) - dot_nt3(bb_im_t, cp_im_t))
        if m >= 1:
            j = m - 1
            q_ref[0, 0:SLAB_STATES, j * rows:(j + 1) * rows] = cp_re_t[0].T.astype(bf16)
            q_ref[0, SLAB_STATES:, j * rows:(j + 1) * rows] = (-cp_im_t[0]).T.astype(bf16)
    for i in range(SSM_Q):
        for j in range(SSM_Q):
            blk = t_blocks[j - i] if j >= i else jnp.zeros((rows, rows), f32)
            t_ref[0, i * rows:(i + 1) * rows, j * rows:(j + 1) * rows] = blk.astype(bf16)

    lq_re, lq_im = _lam_power(SSM_Q, arc_ref[0], aic_ref[0], jnp.exp(lsc_ref[0]))
    lre_ref[0] = lq_re
    lim_ref[0] = lq_im


def _s5_matrices(log_step, a_re, a_im, b_re, b_im, c_re, c_im, mixer_weights):
    nl = log_step.shape[0]
    n = nl * N_SLABS
    rows, p, hg = SLAB_GROUPS * SSM_GROUP, SSM_STATE, SSM_GROUP
    per_row = lambda v: jnp.repeat(v, hg, axis=1).reshape(n, rows, -1)
    by_row = lambda v: v.reshape(n, rows, p)
    compact = lambda v: v.reshape(n, SLAB_GROUPS, -1)

    def spec(*dims):
        return pl.BlockSpec((1,) + dims, lambda i: (i,) + (0,) * len(dims))

    flat = [w.reshape(-1, w.shape[-1]) for w in mixer_weights]
    cast_specs = [pl.BlockSpec((w.shape[0] // n, w.shape[1]), lambda i: (i, 0)) for w in flat]

    wide = 2 * SLAB_STATES
    emat, qmat, tmat, l_re, l_im, *cast = pl.pallas_call(
        functools.partial(_s5_matrix_kernel, n_cast=len(flat)),
        grid=(n,),
        in_specs=[spec(rows, 1), spec(rows, p), spec(rows, p), spec(rows, p), spec(rows, p),
                  spec(rows, p), spec(rows, p), spec(SLAB_GROUPS, 1), spec(SLAB_GROUPS, p),
                  spec(SLAB_GROUPS, p)] + cast_specs,
        out_specs=[spec(SSM_Q * rows, wide), spec(wide, SSM_Q * rows), spec(SSM_Q * rows, SSM_Q * rows),
                   spec(SLAB_GROUPS, p), spec(SLAB_GROUPS, p)] + cast_specs,
        out_shape=[jax.ShapeDtypeStruct((n, SSM_Q * rows, wide), bf16),
                   jax.ShapeDtypeStruct((n, wide, SSM_Q * rows), bf16),
                   jax.ShapeDtypeStruct((n, SSM_Q * rows, SSM_Q * rows), bf16),
                   jax.ShapeDtypeStruct((n, SLAB_GROUPS, p), f32),
                   jax.ShapeDtypeStruct((n, SLAB_GROUPS, p), f32)]
                  + [jax.ShapeDtypeStruct(w.shape, bf16) for w in flat],
        compiler_params=_params(),
        name="s5_matrices",
    )(per_row(log_step[:, :, None]), per_row(a_re), per_row(a_im),
      by_row(jnp.swapaxes(b_re, 2, 3)), by_row(jnp.swapaxes(b_im, 2, 3)), by_row(c_re), by_row(c_im),
      compact(log_step), compact(a_re), compact(a_im), *flat)
    split = lambda m: m.reshape((nl, N_SLABS) + m.shape[1:])
    lam = lambda v: v.reshape(nl, 1, N_SSM_GROUPS * SSM_STATE)
    cast = tuple(c.reshape(w.shape) for c, w in zip(cast, mixer_weights))
    return split(emat), split(qmat), split(tmat), lam(l_re), lam(l_im), cast


def _even_kernel(x_ref, g_ref, win_ref, cw_ref, lr_ref, li_ref, em_ref, qm_ref, tm_ref,
                 dsk_ref, gw_ref, gb_ref, wout_ref, fup_ref, fdn_ref,
                 o_ref, fup16_ref, fdn16_ref, cx_ref, bu_ref, st_ref,
                 xt_ref=None, *, depth_idx, layer_idx):
    i = pl.program_id(0)
    fup16_ref[...] = fup_ref[...].astype(bf16)
    fdn16_ref[...] = fdn_ref[...].astype(bf16)
    gain = g_ref[depth_idx:depth_idx + 1, :]
    d_skip = dsk_ref[layer_idx:layer_idx + 1, :]
    glu_b = gb_ref[layer_idx:layer_idx + 1, :]
    tile_rows = o_ref.shape[0]

    @pl.when(i == 0)
    def _():
        cx_ref[0:CONV_HALO, :] = jnp.zeros((CONV_HALO, D_CONV), f32)
        st_ref[...] = jnp.zeros_like(st_ref)

    if xt_ref is not None:
        for b in range(BATCH):
            for c in range(D_MODEL // LANES):
                xt_ref[c, pl.ds(b, tile_rows // BATCH, stride=BATCH), :] = (
                    x_ref[b, :, c * LANES:(c + 1) * LANES])

    rows = EVEN_SUB_TQ * BATCH
    tq = EVEN_SUB_TQ
    n_sub = tile_rows // rows
    assert D_CONV == D_SSM and N_SLABS == 4

    def normed(sub):
        r_lo = sub * rows
        if xt_ref is None:
            x = x_ref[r_lo:r_lo + rows, :]
        else:
            x = jnp.concatenate(
                [xt_ref[c, r_lo:r_lo + rows, :] for c in range(D_MODEL // LANES)], axis=1)
        return x, _rms(x, gain).astype(bf16)

    def in_slice(h, q):
        return _dot(h, win_ref[:, q * D_CONV:(q + 1) * D_CONV])

    x, h = normed(0)
    xa, ba, ca, u = [in_slice(h, q) for q in range(4)]
    ya = _even_conv(xa, ba, ca, cw_ref, cx_ref)
    u4 = _fold_time(u)
    for s in range(N_SLABS):
        _s5_expand(u4[s], s, em_ref, bu_ref)

    for sub in range(n_sub):
        last = sub + 1 == n_sub
        if not last:
            x_n, h_n = normed(sub + 1)
            parts_n = []
        y4 = []
        for s in range(N_SLABS):
            _s5_scan(s, tq // SSM_Q, lr_ref, li_ref, bu_ref, st_ref)
            if not last:
                parts_n.append(in_slice(h_n, s))
            y4.append(_s5_readout(s, u4[s], qm_ref, tm_ref, bu_ref))
        y = jax.nn.gelu(_unfold_time(y4) + d_skip * u)
        if not last:
            ya_n = _even_conv(parts_n[0], parts_n[1], parts_n[2], cw_ref, cx_ref)
            u_n = parts_n[3]
            u4_n = _fold_time(u_n)
            for s in range(N_SLABS // 2):
                _s5_expand(u4_n[s], s, em_ref, bu_ref)
        gate = _dot(y.astype(bf16), gw_ref[...]) + glu_b
        if not last:
            for s in range(N_SLABS // 2, N_SLABS):
                _s5_expand(u4_n[s], s, em_ref, bu_ref)
        yb = y * jax.nn.sigmoid(gate)
        mix = _dot(ya.astype(bf16), wout_ref[0:D_CONV, :]) + _dot(yb.astype(bf16), wout_ref[D_CONV:, :])
        o_ref[sub * rows:(sub + 1) * rows, :] = x + mix
        if not last:
            x, ya, u, u4 = x_n, ya_n, u_n, u4_n


def _even_conv(xa, ba, ca, cw_ref, cx_ref):
    rows = xa.shape[0]
    cx_ref[CONV_HALO:CONV_HALO + rows, :] = ca * xa
    conv = cx_ref[0:rows, :] * cw_ref[0:1, :]
    for k in range(1, CONV_WIDTH):
        conv = conv + cx_ref[k * BATCH:k * BATCH + rows, :] * cw_ref[k:k + 1, :]
    ya = ba * conv
    cx_ref[0:CONV_HALO, :] = cx_ref[rows:rows + CONV_HALO, :]
    return ya


def _fold_time(u):
    n_chunks = u.shape[0] // (SSM_Q * BATCH)
    at_offset = [
        jnp.concatenate([u[(SSM_Q * k + i) * BATCH:(SSM_Q * k + i + 1) * BATCH, :]
                         for k in range(n_chunks)], axis=0)
        for i in range(SSM_Q)]
    return [jnp.concatenate([a[:, s * LANES:(s + 1) * LANES] for a in at_offset], axis=1).astype(bf16)
            for s in range(N_SLABS)]


def _unfold_time(y4):
    n_chunks = y4[0].shape[0] // BATCH
    at_offset = [jnp.concatenate([y[:, j * LANES:(j + 1) * LANES] for y in y4], axis=1)
                 for j in range(SSM_Q)]
    return jnp.concatenate([at_offset[j][k * BATCH:(k + 1) * BATCH, :]
                            for k in range(n_chunks) for j in range(SSM_Q)], axis=0)


def _s5_expand(u4s, s, em_ref, bu_ref):
    c0 = 2 * SLAB_STATES * s
    bu_ref[:, c0:c0 + 2 * SLAB_STATES] = _dot(u4s, em_ref[s])


def _s5_scan(s, n_chunks, lr_ref, li_ref, bu_ref, st_ref):
    c_re = 2 * SLAB_STATES * s
    c_im = c_re + SLAB_STATES
    re = slice(c_re, c_re + SLAB_STATES)
    im = slice(c_im, c_im + SLAB_STATES)
    lam = slice(SLAB_STATES * s, SLAB_STATES * (s + 1))
    ar = jnp.broadcast_to(lr_ref[:, lam], (BATCH, SLAB_STATES))
    ai = jnp.broadcast_to(li_ref[:, lam], (BATCH, SLAB_STATES))
    s_re = st_ref[:, re]
    s_im = st_ref[:, im]
    for k in range(n_chunks):
        r = slice(k * BATCH, (k + 1) * BATCH)
        b_re = bu_ref[r, re]
        b_im = bu_ref[r, im]
        bu_ref[r, re] = s_re
        bu_ref[r, im] = s_im
        s_re, s_im = ar * s_re - ai * s_im + b_re, ar * s_im + ai * s_re + b_im
    st_ref[:, re] = s_re
    st_ref[:, im] = s_im


def _s5_readout(s, u4s, qm_ref, tm_ref, bu_ref):
    c_re = 2 * SLAB_STATES * s
    c_im = c_re + SLAB_STATES
    return (_dot(bu_ref[:, c_re:c_re + SLAB_STATES].astype(bf16), qm_ref[s, 0:SLAB_STATES, :])
            + _dot(bu_ref[:, c_im:c_im + SLAB_STATES].astype(bf16), qm_ref[s, SLAB_STATES:, :])
            + _dot(u4s, tm_ref[s]))


def _even_mixer(x, i, j, layer_params, ffn_w_up, ffn_w_down, batch_major_in):
    tq = EVEN_TQ_BATCH_MAJOR if batch_major_in else EVEN_TQ
    rows = tq * BATCH
    cast_in, cast_out, cast_shapes = _ffn_weight_cast_specs(N_ROWS // rows, i)
    sub_rows = EVEN_SUB_TQ * BATCH
    x_spec = _batch_rows(tq, D_MODEL) if batch_major_in else _rows(rows, D_MODEL)
    scratch = [
        pltpu.VMEM((sub_rows + CONV_HALO, D_CONV), f32),
        pltpu.VMEM((sub_rows // SSM_Q, SSM_COLS), f32),
        pltpu.VMEM((BATCH, SSM_COLS), f32),
    ]
    if batch_major_in:
        scratch.append(pltpu.VMEM((D_MODEL // LANES, rows, LANES), f32))
    return pl.pallas_call(
        functools.partial(_even_kernel, depth_idx=i, layer_idx=j),
        grid=(N_ROWS // rows,),
        in_specs=[x_spec] + [_layer_spec(a, j) for a in layer_params] + cast_in,
        out_specs=[_rows(rows, D_MODEL)] + cast_out,
        out_shape=[jax.ShapeDtypeStruct((N_ROWS, D_MODEL), f32)] + cast_shapes,
        scratch_shapes=scratch,
        compiler_params=_params(),
        name="even_mixer",
    )(x, *layer_params, ffn_w_up, ffn_w_down)


def _odd_kernel(x_ref, g_ref, win_ref, pw_ref, ps_ref, ng_ref, sw_ref, sb_ref, wout_ref,
                fup_ref, fdn_ref, o_ref, fup16_ref, fdn16_ref, z_ref, v_ref, m_ref,
                *, depth_idx, layer_idx):
    i = pl.program_id(0)
    fup16_ref[...] = fup_ref[...].astype(bf16)
    fdn16_ref[...] = fdn_ref[...].astype(bf16)
    tq = CHUNK
    rows = tq * BATCH
    n_chunks = x_ref.shape[0] // rows

    @pl.when(i == 0)
    def _():
        z_ref[0:POOL_HALO, :] = jnp.zeros((POOL_HALO, D_POOL), f32)

    def front(c):
        x = x_ref[c * rows:(c + 1) * rows, :]
        h = _rms(x, g_ref[depth_idx:depth_idx + 1, :]).astype(bf16)
        proj = _dot(h, win_ref[...])
        uv = jax.nn.gelu(proj[:, D_POOL:])
        return x, proj[:, 0:D_POOL], uv[:, 0:D_SGU], uv[:, D_SGU:]

    cur = front(0)
    for c in range(n_chunks):
        nxt = front(c + 1) if c + 1 < n_chunks else None
        x, z, su, sv = cur
        mix = _odd_mix(i * n_chunks + c, z, su, sv, pw_ref, ps_ref, ng_ref, sw_ref, sb_ref, wout_ref,
                       z_ref, v_ref, m_ref, layer_idx)
        o_ref[c * rows:(c + 1) * rows, :] = x + mix
        cur = nxt


def _odd_mix(chunk_idx, z, su, sv, pw_ref, ps_ref, ng_ref, sw_ref, sb_ref, wout_ref,
             z_ref, v_ref, m_ref, layer_idx):
    rows = z.shape[0]
    tq = rows // BATCH

    z_ref[POOL_HALO:POOL_HALO + rows, :] = z
    count = (chunk_idx * tq + 1 + lax.broadcasted_iota(jnp.int32, (rows, 1), 0) // BATCH).astype(f32)
    yc = []
    for gi, win in enumerate(POOL_WINDOWS):
        cols = slice(gi * POOL_GROUP, (gi + 1) * POOL_GROUP)
        e = z_ref[:, cols]
        span = 1
        while span < win:
            sh = span * BATCH
            e = e[sh:, :] + e[:-sh, :]
            span *= 2
        wsum = e[e.shape[0] - rows:, :]
        pooled = wsum / jnp.minimum(count, float(win)) - z[:, cols]
        yc.append(_dot(pooled.astype(bf16), pw_ref[gi]))
    yc = jnp.concatenate(yc, axis=1) * ps_ref[layer_idx:layer_idx + 1, :]
    z_ref[0:POOL_HALO, :] = z_ref[rows:rows + POOL_HALO, :]

    v = _rms(sv, ng_ref[layer_idx:layer_idx + 1, :])
    tri = lax.broadcasted_iota(jnp.int32, (CHUNK, CHUNK), 0) >= lax.broadcasted_iota(
        jnp.int32, (CHUNK, CHUNK), 1)
    for hd in range(SGU_HEADS):
        v_ref[hd] = v[:, hd * SGU_HEAD_DIM:(hd + 1) * SGU_HEAD_DIM]
    for hd in range(SGU_HEADS):
        w_s = jnp.where(tri, sw_ref[hd], 0.0).astype(bf16)
        v_b = jnp.concatenate(
            [v_ref[hd, pl.ds(b, CHUNK, stride=BATCH), :].astype(bf16) for b in range(BATCH)],
            axis=1)
        mixed_b = _dot(w_s, v_b) + sb_ref[:, hd:hd + 1]
        for b in range(BATCH):
            m_ref[hd, pl.ds(b, CHUNK, stride=BATCH), :] = mixed_b[:, b * SGU_HEAD_DIM:(b + 1) * SGU_HEAD_DIM]
    yd = su * jnp.concatenate([m_ref[hd] for hd in range(SGU_HEADS)], axis=1)

    return _dot(yc.astype(bf16), wout_ref[0:D_POOL, :]) + _dot(yd.astype(bf16), wout_ref[D_POOL:, :])


def _odd_mixer(x, i, j, layer_params, ffn_w_up, ffn_w_down):
    rows = ODD_TQ * BATCH
    chunk_rows = CHUNK * BATCH
    cast_in, cast_out, cast_shapes = _ffn_weight_cast_specs(N_ROWS // rows, i)
    return pl.pallas_call(
        functools.partial(_odd_kernel, depth_idx=i, layer_idx=j),
        grid=(N_ROWS // rows,),
        in_specs=[_rows(rows, D_MODEL)] + [_layer_spec(a, j) for a in layer_params] + cast_in,
        out_specs=[_rows(rows, D_MODEL)] + cast_out,
        out_shape=[jax.ShapeDtypeStruct((N_ROWS, D_MODEL), f32)] + cast_shapes,
        scratch_shapes=[
            pltpu.VMEM((chunk_rows + POOL_HALO, D_POOL), f32),
            pltpu.VMEM((SGU_HEADS, chunk_rows, SGU_HEAD_DIM), f32),
            pltpu.VMEM((SGU_HEADS, chunk_rows, SGU_HEAD_DIM), f32),
        ],
        compiler_params=_params(),
        name="odd_mixer",
    )(x, *layer_params, ffn_w_up, ffn_w_down)


def _ffn_chunks():
    out, c0 = [], 0
    while c0 < D_FF:
        cw = min(256, D_FF - c0)
        out.append((c0, cw))
        c0 += cw
    return out


def _ffn_kernel(x_ref, g_ref, wup_ref, cw_ref, cb_ref, wdn_ref, gf_ref, o_ref, up_ref, act_ref,
                yt_ref=None, *, depth_idx):
    i = pl.program_id(0)
    rows = x_ref.shape[0]
    tq = rows // BATCH

    @pl.when(i == 0)
    def _():
        up_ref[...] = jnp.zeros_like(up_ref)

    x = x_ref[...]
    h = _rms(x, g_ref[depth_idx:depth_idx + 1, :]).astype(bf16)

    def conv(off, cw):
        cols = slice(off, off + cw)
        up = _dot(h, wup_ref[:, cols])
        ext = jnp.concatenate([up_ref[:, cols], up], axis=0)
        up_ref[:, cols] = up[rows - CONV_HALO:rows, :]
        y = ext[0:rows, :] * cw_ref[0:1, cols]
        for k in range(1, CONV_WIDTH):
            y = y + ext[k * BATCH:k * BATCH + rows, :] * cw_ref[k:k + 1, cols]
        return y + cb_ref[depth_idx:depth_idx + 1, cols]

    for c0, cw in _ffn_chunks():
        gate = conv(c0, cw)
        val = conv(D_FF + c0, cw)
        act_ref[:, c0:c0 + cw] = (jax.nn.silu(gate) * val).astype(bf16)

    out = x + _dot(act_ref[...], wdn_ref[...])
    if yt_ref is None:
        o_ref[...] = out
    else:
        out = _rms(out, gf_ref[...])
        for c in range(D_MODEL // LANES):
            yt_ref[c] = out[:, c * LANES:(c + 1) * LANES]
        for b in range(BATCH):
            for c in range(D_MODEL // LANES):
                o_ref[b, :, c * LANES:(c + 1) * LANES] = yt_ref[c, pl.ds(b, tq, stride=BATCH), :]


def _conv_ffn(x, j, layer_params, g_final, final):
    rows = FFN_TQ * BATCH
    scratch = [pltpu.VMEM((CONV_HALO, 2 * D_FF), f32), pltpu.VMEM((rows, D_FF), bf16)]
    if final:
        scratch.append(pltpu.VMEM((D_MODEL // LANES, rows, LANES), f32))
        out_spec = _batch_rows(FFN_TQ, D_MODEL)
        out_shape = jax.ShapeDtypeStruct((BATCH, SEQ, D_MODEL), f32)
    else:
        out_spec = _rows(rows, D_MODEL)
        out_shape = jax.ShapeDtypeStruct((N_ROWS, D_MODEL), f32)
    return pl.pallas_call(
        functools.partial(_ffn_kernel, depth_idx=j),
        grid=(N_ROWS // rows,),
        in_specs=([_rows(rows, D_MODEL)] + [_layer_spec(a, j) for a in layer_params]
                  + [_layer_spec(g_final, 0)]),
        out_specs=out_spec,
        out_shape=out_shape,
        scratch_shapes=scratch,
        compiler_params=_params(),
        name="conv_ffn",
    )(x, *layer_params, g_final)


def kernel(x, norm_mix_g, even_w_in, even_conv_w, ssm_log_step, ssm_a_re, ssm_a_im, ssm_b_re, ssm_b_im, ssm_c_re, ssm_c_im, ssm_d, ssm_glu_w, ssm_glu_b, even_w_out, odd_w_in, pool_w, pool_scale, sgu_norm_g, sgu_w, sgu_b, odd_w_out, norm_ffn_g, ffn_w_up, ffn_conv_w, ffn_conv_b, ffn_w_down, norm_final_g):
    assert x.shape == (BATCH, SEQ, D_MODEL)
    n_odd = pool_w.shape[0]
    emat, qmat, tmat, l_re, l_im, cast = _s5_matrices(
        ssm_log_step, ssm_a_re, ssm_a_im, ssm_b_re, ssm_b_im, ssm_c_re, ssm_c_im,
        (even_w_in, ssm_glu_w, even_w_out, odd_w_in,
         pool_w.reshape(n_odd, len(POOL_WINDOWS) * POOL_GROUP, POOL_GROUP), odd_w_out))
    even_w_in16, glu_w16, even_w_out16, odd_w_in16, pool_w16, odd_w_out16 = cast
    even_params = (
        norm_mix_g, even_w_in16, even_conv_w,
        l_re, l_im, emat, qmat, tmat, ssm_d,
        glu_w16, ssm_glu_b, even_w_out16)

    odd_params = (
        norm_mix_g, odd_w_in16, pool_w16.reshape(pool_w.shape),
        pool_scale, sgu_norm_g, sgu_w, jnp.swapaxes(sgu_b, 1, 2),
        odd_w_out16)

    g_final = norm_final_g.reshape(1, D_MODEL)

    xt = x
    for i in range(DEPTH):
        j = i // 2
        if i % 2 == 0:
            xt, w_up, w_down = _even_mixer(xt, i, j, even_params, ffn_w_up, ffn_w_down,
                                           batch_major_in=(i == 0))
        else:
            xt, w_up, w_down = _odd_mixer(xt, i, j, odd_params, ffn_w_up, ffn_w_down)
        ffn_params = (norm_ffn_g, w_up, ffn_conv_w, ffn_conv_b, w_down)
        xt = _conv_ffn(xt, i, ffn_params, g_final, final=(i == DEPTH - 1))
    return xt
```

```python
import functools

import jax
import jax.numpy as jnp
from jax import lax
from jax.experimental import pallas as pl
from jax.experimental.pallas import tpu as pltpu

D_MODEL = 1024
BATCH = 8
SEQ = 4096
DEPTH = 4
D_CONV = D_MODEL // 2
CONV_WIDTH = 3
D_SSM = D_MODEL // 2
SSM_GROUP = 16
N_SSM_GROUPS = D_SSM // SSM_GROUP
SSM_STATE = 64
D_POOL = D_MODEL // 2
POOL_WINDOWS = (2, 4, 8, 16)
POOL_GROUP = D_POOL // len(POOL_WINDOWS)
D_SGU = D_MODEL // 2
SGU_HEADS = 4
SGU_HEAD_DIM = D_SGU // SGU_HEADS
CHUNK = 128
D_FF = ((8 * D_MODEL // 3 + 127) // 128) * 128
EPS = 1e-6

N_ROWS = SEQ * BATCH
LANES = 128
V7X_VMEM_BYTES = 64 * 1024 * 1024
VMEM_LIMIT_BYTES = V7X_VMEM_BYTES - 8 * 1024 * 1024

CONV_HALO = (CONV_WIDTH - 1) * BATCH
POOL_HALO = max(POOL_WINDOWS) * BATCH
SSM_Q = 4
SLAB_GROUPS = LANES // SSM_GROUP
N_SLABS = N_SSM_GROUPS // SLAB_GROUPS
SLAB_STATES = SLAB_GROUPS * SSM_STATE
SSM_COLS = 2 * N_SLABS * SLAB_STATES

EVEN_TQ = 128
EVEN_TQ_BATCH_MAJOR = 128
EVEN_SUB_TQ = 128
FFN_TQ = 128
FFN_CAST_BLOCKS = 16
ODD_TQ = CHUNK

f32 = jnp.float32
bf16 = jnp.bfloat16


def _rms(x, g):
    return x * lax.rsqrt(jnp.mean(x * x, axis=-1, keepdims=True) + EPS) * g


def _dot(a, b):
    return jnp.dot(a, b, preferred_element_type=f32)


def _layer_spec(arr, j):
    nd = arr.ndim
    if nd == 2:
        return pl.BlockSpec(arr.shape, lambda i: (0, 0))
    return pl.BlockSpec((None,) + arr.shape[1:], lambda i: (j,) + (0,) * (nd - 1),
                        pipeline_mode=pl.Buffered(1))


def _rows(r, d):
    return pl.BlockSpec((r, d), lambda i: (i, 0))


def _batch_rows(tq, d):
    return pl.BlockSpec((BATCH, tq, d), lambda i: (0, i, 0))


def _ffn_weight_cast_specs(n_steps, layer):
    up_rows, dn_rows = D_MODEL // FFN_CAST_BLOCKS, D_FF // FFN_CAST_BLOCKS
    blk = lambda i: i * FFN_CAST_BLOCKS // n_steps
    in_specs = [pl.BlockSpec((None, up_rows, 2 * D_FF), lambda i: (layer, blk(i), 0)),
                pl.BlockSpec((None, dn_rows, D_MODEL), lambda i: (layer, blk(i), 0))]
    out_specs = [pl.BlockSpec((up_rows, 2 * D_FF), lambda i: (blk(i), 0)),
                 pl.BlockSpec((dn_rows, D_MODEL), lambda i: (blk(i), 0))]
    out_shapes = [jax.ShapeDtypeStruct((D_MODEL, 2 * D_FF), bf16),
                  jax.ShapeDtypeStruct((D_FF, D_MODEL), bf16)]
    return in_specs, out_specs, out_shapes


def _params():
    return pltpu.CompilerParams(dimension_semantics=("arbitrary",),
                                vmem_limit_bytes=VMEM_LIMIT_BYTES)


def _lam_power(m, a_re, a_im, step):
    mag = jnp.exp((m * a_re) * step)
    ang = (m * a_im) * step
    return mag * jnp.cos(ang), mag * jnp.sin(ang)


def _s5_matrix_kernel(lsr_ref, arr_ref, air_ref, btre_ref, btim_ref, cre_ref, cim_ref,
                      lsc_ref, arc_ref, aic_ref, *refs, n_cast):
    e_ref, q_ref, t_ref, lre_ref, lim_ref = refs[n_cast:n_cast + 5]
    for src, dst in zip(refs[:n_cast], refs[n_cast + 5:]):
        dst[...] = src[...].astype(bf16)

    rows, p = SLAB_GROUPS * SSM_GROUP, SSM_STATE
    step = jnp.exp(lsr_ref[0])
    a_re = arr_ref[0]
    a_im = air_ref[0]
    power = lambda m: _lam_power(m, a_re, a_im, step)

    l_re, l_im = power(1)
    n_re = l_re - 1.0
    den = a_re * a_re + a_im * a_im
    k_re = (n_re * a_re + l_im * a_im) / den
    k_im = (l_im * a_re - n_re * a_im) / den
    bt_re = btre_ref[0]
    bt_im = btim_ref[0]
    bb_re = k_re * bt_re - k_im * bt_im
    bb_im = k_re * bt_im + k_im * bt_re
    c_re = cre_ref[0]
    c_im = cim_ref[0]

    col = lax.broadcasted_iota(jnp.int32, (p, SLAB_STATES), 1)
    spread = (lax.bitwise_and(col, p - 1) == lax.broadcasted_iota(jnp.int32, (p, SLAB_STATES), 0)
              ).astype(bf16)
    same_group = (
        lax.shift_right_logical(lax.broadcasted_iota(jnp.int32, (rows, SLAB_STATES), 0),
                                SSM_GROUP.bit_length() - 1)
        == lax.shift_right_logical(lax.broadcasted_iota(jnp.int32, (rows, SLAB_STATES), 1),
                                   SSM_STATE.bit_length() - 1))

    def tile(v16):
        return jnp.where(same_group, _dot(v16, spread), 0.0)

    def split(v):
        hi = v.astype(bf16)
        return hi, (v - hi.astype(f32)).astype(bf16)

    def tile_split(v):
        hi, lo = split(v)
        return tile(hi), tile(lo)

    def dot_nt(a, b):
        return lax.dot_general(a.astype(bf16), b.astype(bf16), (((1,), (1,)), ((), ())),
                               preferred_element_type=f32)

    def dot_nt3(a, b):
        return dot_nt(a[0], b[0]) + dot_nt(a[0], b[1]) + dot_nt(a[1], b[0])

    bb_re_t, bb_im_t = tile_split(bb_re), tile_split(bb_im)
    t_blocks = []
    for m in range(SSM_Q + 1):
        p_re, p_im = power(m)
        cp_re_t = tile_split(c_re * p_re - c_im * p_im)
        cp_im_t = tile_split(c_re * p_im + c_im * p_re)
        if m < SSM_Q:
            i = SSM_Q - 1 - m
            e_ref[0, i * rows:(i + 1) * rows, 0:SLAB_STATES] = (
                tile((p_re * bb_re - p_im * bb_im).astype(bf16)).astype(bf16))
            e_ref[0, i * rows:(i + 1) * rows, SLAB_STATES:] = (
                tile((p_re * bb_im + p_im * bb_re).astype(bf16)).astype(bf16))
            t_blocks.append(dot_nt3(bb_re_t, cp_re_t) - dot_nt3(bb_im_t, cp_im_t))
        if m >= 1:
            j = m - 1
            q_ref[0, 0:SLAB_STATES, j * rows:(j + 1) * rows] = cp_re_t[0].T.astype(bf16)
            q_ref[0, SLAB_STATES:, j * rows:(j + 1) * rows] = (-cp_im_t[0]).T.astype(bf16)
    for i in range(SSM_Q):
        for j in range(SSM_Q):
            blk = t_blocks[j - i] if j >= i else jnp.zeros((rows, rows), f32)
            t_ref[0, i * rows:(i + 1) * rows, j * rows:(j + 1) * rows] = blk.astype(bf16)

    lq_re, lq_im = _lam_power(SSM_Q, arc_ref[0], aic_ref[0], jnp.exp(lsc_ref[0]))
    lre_ref[0] = lq_re
    lim_ref[0] = lq_im


def _s5_matrices(log_step, a_re, a_im, b_re, b_im, c_re, c_im, mixer_weights):
    nl = log_step.shape[0]
    n = nl * N_SLABS
    rows, p, hg = SLAB_GROUPS * SSM_GROUP, SSM_STATE, SSM_GROUP
    per_row = lambda v: jnp.repeat(v, hg, axis=1).reshape(n, rows, -1)
    by_row = lambda v: v.reshape(n, rows, p)
    compact = lambda v: v.reshape(n, SLAB_GROUPS, -1)

    def spec(*dims):
        return pl.BlockSpec((1,) + dims, lambda i: (i,) + (0,) * len(dims))

    flat = [w.reshape(-1, w.shape[-1]) for w in mixer_weights]
    cast_specs = [pl.BlockSpec((w.shape[0] // n, w.shape[1]), lambda i: (i, 0)) for w in flat]

    wide = 2 * SLAB_STATES
    emat, qmat, tmat, l_re, l_im, *cast = pl.pallas_call(
        functools.partial(_s5_matrix_kernel, n_cast=len(flat)),
        grid=(n,),
        in_specs=[spec(rows, 1), spec(rows, p), spec(rows, p), spec(rows, p), spec(rows, p),
                  spec(rows, p), spec(rows, p), spec(SLAB_GROUPS, 1), spec(SLAB_GROUPS, p),
                  spec(SLAB_GROUPS, p)] + cast_specs,
        out_specs=[spec(SSM_Q * rows, wide), spec(wide, SSM_Q * rows), spec(SSM_Q * rows, SSM_Q * rows),
                   spec(SLAB_GROUPS, p), spec(SLAB_GROUPS, p)] + cast_specs,
        out_shape=[jax.ShapeDtypeStruct((n, SSM_Q * rows, wide), bf16),
                   jax.ShapeDtypeStruct((n, wide, SSM_Q * rows), bf16),
                   jax.ShapeDtypeStruct((n, SSM_Q * rows, SSM_Q * rows), bf16),
                   jax.ShapeDtypeStruct((n, SLAB_GROUPS, p), f32),
                   jax.ShapeDtypeStruct((n, SLAB_GROUPS, p), f32)]
                  + [jax.ShapeDtypeStruct(w.shape, bf16) for w in flat],
        compiler_params=_params(),
        name="s5_matrices",
    )(per_row(log_step[:, :, None]), per_row(a_re), per_row(a_im),
      by_row(jnp.swapaxes(b_re, 2, 3)), by_row(jnp.swapaxes(b_im, 2, 3)), by_row(c_re), by_row(c_im),
      compact(log_step), compact(a_re), compact(a_im), *flat)
    split = lambda m: m.reshape((nl, N_SLABS) + m.shape[1:])
    lam = lambda v: v.reshape(nl, 1, N_SSM_GROUPS * SSM_STATE)
    cast = tuple(c.reshape(w.shape) for c, w in zip(cast, mixer_weights))
    return split(emat), split(qmat), split(tmat), lam(l_re), lam(l_im), cast


def _even_kernel(x_ref, g_ref, win_ref, cw_ref, lr_ref, li_ref, em_ref, qm_ref, tm_ref,
                 dsk_ref, gw_ref, gb_ref, wout_ref, fup_ref, fdn_ref,
                 o_ref, fup16_ref, fdn16_ref, cx_ref, bu_ref, st_ref,
                 xt_ref=None, *, depth_idx, layer_idx):
    i = pl.program_id(0)
    fup16_ref[...] = fup_ref[...].astype(bf16)
    fdn16_ref[...] = fdn_ref[...].astype(bf16)
    gain = g_ref[depth_idx:depth_idx + 1, :]
    d_skip = dsk_ref[layer_idx:layer_idx + 1, :]
    glu_b = gb_ref[layer_idx:layer_idx + 1, :]
    tile_rows = o_ref.shape[0]

    @pl.when(i == 0)
    def _():
        cx_ref[0:CONV_HALO, :] = jnp.zeros((CONV_HALO, D_CONV), f32)
        st_ref[...] = jnp.zeros_like(st_ref)

    if xt_ref is not None:
        for b in range(BATCH):
            for c in range(D_MODEL // LANES):
                xt_ref[c, pl.ds(b, tile_rows // BATCH, stride=BATCH), :] = (
                    x_ref[b, :, c * LANES:(c + 1) * LANES])

    rows = EVEN_SUB_TQ * BATCH
    tq = EVEN_SUB_TQ
    n_sub = tile_rows // rows
    assert D_CONV == D_SSM and N_SLABS == 4

    def normed(sub):
        r_lo = sub * rows
        if xt_ref is None:
            x = x_ref[r_lo:r_lo + rows, :]
        else:
            x = jnp.concatenate(
                [xt_ref[c, r_lo:r_lo + rows, :] for c in range(D_MODEL // LANES)], axis=1)
        return x, _rms(x, gain).astype(bf16)

    def in_slice(h, q):
        return _dot(h, win_ref[:, q * D_CONV:(q + 1) * D_CONV])

    x, h = normed(0)
    xa, ba, ca, u = [in_slice(h, q) for q in range(4)]
    ya = _even_conv(xa, ba, ca, cw_ref, cx_ref)
    u4 = _fold_time(u)
    for s in range(N_SLABS):
        _s5_expand(u4[s], s, em_ref, bu_ref)

    for sub in range(n_sub):
        last = sub + 1 == n_sub
        if not last:
            x_n, h_n = normed(sub + 1)
            parts_n = []
        y4 = []
        for s in range(N_SLABS):
            _s5_scan(s, tq // SSM_Q, lr_ref, li_ref, bu_ref, st_ref)
            if not last:
                parts_n.append(in_slice(h_n, s))
            y4.append(_s5_readout(s, u4[s], qm_ref, tm_ref, bu_ref))
        y = jax.nn.gelu(_unfold_time(y4) + d_skip * u)
        if not last:
            ya_n = _even_conv(parts_n[0], parts_n[1], parts_n[2], cw_ref, cx_ref)
            u_n = parts_n[3]
            u4_n = _fold_time(u_n)
            for s in range(N_SLABS // 2):
                _s5_expand(u4_n[s], s, em_ref, bu_ref)
        gate = _dot(y.astype(bf16), gw_ref[...]) + glu_b
        if not last:
            for s in range(N_SLABS // 2, N_SLABS):
                _s5_expand(u4_n[s], s, em_ref, bu_ref)
        yb = y * jax.nn.sigmoid(gate)
        mix = _dot(ya.astype(bf16), wout_ref[0:D_CONV, :]) + _dot(yb.astype(bf16), wout_ref[D_CONV:, :])
        o_ref[sub * rows:(sub + 1) * rows, :] = x + mix
        if not last:
            x, ya, u, u4 = x_n, ya_n, u_n, u4_n


def _even_conv(xa, ba, ca, cw_ref, cx_ref):
    rows = xa.shape[0]
    cx_ref[CONV_HALO:CONV_HALO + rows, :] = ca * xa
    conv = cx_ref[0:rows, :] * cw_ref[0:1, :]
    for k in range(1, CONV_WIDTH):
        conv = conv + cx_ref[k * BATCH:k * BATCH + rows, :] * cw_ref[k:k + 1, :]
    ya = ba * conv
    cx_ref[0:CONV_HALO, :] = cx_ref[rows:rows + CONV_HALO, :]
    return ya


def _fold_time(u):
    n_chunks = u.shape[0] // (SSM_Q * BATCH)
    at_offset = [
        jnp.concatenate([u[(SSM_Q * k + i) * BATCH:(SSM_Q * k + i + 1) * BATCH, :]
                         for k in range(n_chunks)], axis=0)
        for i in range(SSM_Q)]
    return [jnp.concatenate([a[:, s * LANES:(s + 1) * LANES] for a in at_offset], axis=1).astype(bf16)
            for s in range(N_SLABS)]


def _unfold_time(y4):
    n_chunks = y4[0].shape[0] // BATCH
    at_offset = [jnp.concatenate([y[:, j * LANES:(j + 1) * LANES] for y in y4], axis=1)
                 for j in range(SSM_Q)]
    return jnp.concatenate([at_offset[j][k * BATCH:(k + 1) * BATCH, :]
                            for k in range(n_chunks) for j in range(SSM_Q)], axis=0)


def _s5_expand(u4s, s, em_ref, bu_ref):
    c0 = 2 * SLAB_STATES * s
    bu_ref[:, c0:c0 + 2 * SLAB_STATES] = _dot(u4s, em_ref[s])


def _s5_scan(s, n_chunks, lr_ref, li_ref, bu_ref, st_ref):
    c_re = 2 * SLAB_STATES * s
    c_im = c_re + SLAB_STATES
    re = slice(c_re, c_re + SLAB_STATES)
    im = slice(c_im, c_im + SLAB_STATES)
    lam = slice(SLAB_STATES * s, SLAB_STATES * (s + 1))
    ar = jnp.broadcast_to(lr_ref[:, lam], (BATCH, SLAB_STATES))
    ai = jnp.broadcast_to(li_ref[:, lam], (BATCH, SLAB_STATES))
    s_re = st_ref[:, re]
    s_im = st_ref[:, im]
    for k in range(n_chunks):
        r = slice(k * BATCH, (k + 1) * BATCH)
        b_re = bu_ref[r, re]
        b_im = bu_ref[r, im]
        bu_ref[r, re] = s_re
        bu_ref[r, im] = s_im
        s_re, s_im = ar * s_re - ai * s_im + b_re, ar * s_im + ai * s_re + b_im
    st_ref[:, re] = s_re
    st_ref[:, im] = s_im


def _s5_readout(s, u4s, qm_ref, tm_ref, bu_ref):
    c_re = 2 * SLAB_STATES * s
    c_im = c_re + SLAB_STATES
    return (_dot(bu_ref[:, c_re:c_re + SLAB_STATES].astype(bf16), qm_ref[s, 0:SLAB_STATES, :])
            + _dot(bu_ref[:, c_im:c_im + SLAB_STATES].astype(bf16), qm_ref[s, SLAB_STATES:, :])
            + _dot(u4s, tm_ref[s]))


def _even_mixer(x, i, j, layer_params, ffn_w_up, ffn_w_down, batch_major_in):
    tq = EVEN_TQ_BATCH_MAJOR if batch_major_in else EVEN_TQ
    rows = tq * BATCH
    cast_in, cast_out, cast_shapes = _ffn_weight_cast_specs(N_ROWS // rows, i)
    sub_rows = EVEN_SUB_TQ * BATCH
    x_spec = _batch_rows(tq, D_MODEL) if batch_major_in else _rows(rows, D_MODEL)
    scratch = [
        pltpu.VMEM((sub_rows + CONV_HALO, D_CONV), f32),
        pltpu.VMEM((sub_rows // SSM_Q, SSM_COLS), f32),
        pltpu.VMEM((BATCH, SSM_COLS), f32),
    ]
    if batch_major_in:
        scratch.append(pltpu.VMEM((D_MODEL // LANES, rows, LANES), f32))
    return pl.pallas_call(
        functools.partial(_even_kernel, depth_idx=i, layer_idx=j),
        grid=(N_ROWS // rows,),
        in_specs=[x_spec] + [_layer_spec(a, j) for a in layer_params] + cast_in,
        out_specs=[_rows(rows, D_MODEL)] + cast_out,
        out_shape=[jax.ShapeDtypeStruct((N_ROWS, D_MODEL), f32)] + cast_shapes,
        scratch_shapes=scratch,
        compiler_params=_params(),
        name="even_mixer",
    )(x, *layer_params, ffn_w_up, ffn_w_down)


def _odd_kernel(x_ref, g_ref, win_ref, pw_ref, ps_ref, ng_ref, sw_ref, sb_ref, wout_ref,
                fup_ref, fdn_ref, o_ref, fup16_ref, fdn16_ref, z_ref, v_ref, m_ref,
                *, depth_idx, layer_idx):
    i = pl.program_id(0)
    fup16_ref[...] = fup_ref[...].astype(bf16)
    fdn16_ref[...] = fdn_ref[...].astype(bf16)
    tq = CHUNK
    rows = tq * BATCH
    n_chunks = x_ref.shape[0] // rows

    @pl.when(i == 0)
    def _():
        z_ref[0:POOL_HALO, :] = jnp.zeros((POOL_HALO, D_POOL), f32)

    def front(c):
        x = x_ref[c * rows:(c + 1) * rows, :]
        h = _rms(x, g_ref[depth_idx:depth_idx + 1, :]).astype(bf16)
        proj = _dot(h, win_ref[...])
        uv = jax.nn.gelu(proj[:, D_POOL:])
        return x, proj[:, 0:D_POOL], uv[:, 0:D_SGU], uv[:, D_SGU:]

    cur = front(0)
    for c in range(n_chunks):
        nxt = front(c + 1) if c + 1 < n_chunks else None
        x, z, su, sv = cur
        mix = _odd_mix(i * n_chunks + c, z, su, sv, pw_ref, ps_ref, ng_ref, sw_ref, sb_ref, wout_ref,
                       z_ref, v_ref, m_ref, layer_idx)
        o_ref[c * rows:(c + 1) * rows, :] = x + mix
        cur = nxt


def _odd_mix(chunk_idx, z, su, sv, pw_ref, ps_ref, ng_ref, sw_ref, sb_ref, wout_ref,
             z_ref, v_ref, m_ref, layer_idx):
    rows = z.shape[0]
    tq = rows // BATCH

    z_ref[POOL_HALO:POOL_HALO + rows, :] = z
    count = (chunk_idx * tq + 1 + lax.broadcasted_iota(jnp.int32, (rows, 1), 0) // BATCH).astype(f32)
    yc = []
    for gi, win in enumerate(POOL_WINDOWS):
        cols = slice(gi * POOL_GROUP, (gi + 1) * POOL_GROUP)
        e = z_ref[:, cols]
        span = 1
        while span < win:
            sh = span * BATCH
            e = e[sh:, :] + e[:-sh, :]
            span *= 2
        wsum = e[e.shape[0] - rows:, :]
        pooled = wsum / jnp.minimum(count, float(win)) - z[:, cols]
        yc.append(_dot(pooled.astype(bf16), pw_ref[gi]))
    yc = jnp.concatenate(yc, axis=1) * ps_ref[layer_idx:layer_idx + 1, :]
    z_ref[0:POOL_HALO, :] = z_ref[rows:rows + POOL_HALO, :]

    v = _rms(sv, ng_ref[layer_idx:layer_idx + 1, :])
    tri = lax.broadcasted_iota(jnp.int32, (CHUNK, CHUNK), 0) >= lax.broadcasted_iota(
        jnp.int32, (CHUNK, CHUNK), 1)
    for hd in range(SGU_HEADS):
        v_ref[hd] = v[:, hd * SGU_HEAD_DIM:(hd + 1) * SGU_HEAD_DIM]
    for hd in range(SGU_HEADS):
        w_s = jnp.where(tri, sw_ref[hd], 0.0).astype(bf16)
        v_b = jnp.concatenate(
            [v_ref[hd, pl.ds(b, CHUNK, stride=BATCH), :].astype(bf16) for b in range(BATCH)],
            axis=1)
        mixed_b = _dot(w_s, v_b) + sb_ref[:, hd:hd + 1]
        for b in range(BATCH):
            m_ref[hd, pl.ds(b, CHUNK, stride=BATCH), :] = mixed_b[:, b * SGU_HEAD_DIM:(b + 1) * SGU_HEAD_DIM]
    yd = su * jnp.concatenate([m_ref[hd] for hd in range(SGU_HEADS)], axis=1)

    return _dot(yc.astype(bf16), wout_ref[0:D_POOL, :]) + _dot(yd.astype(bf16), wout_ref[D_POOL:, :])


def _odd_mixer(x, i, j, layer_params, ffn_w_up, ffn_w_down):
    rows = ODD_TQ * BATCH
    chunk_rows = CHUNK * BATCH
    cast_in, cast_out, cast_shapes = _ffn_weight_cast_specs(N_ROWS // rows, i)
    return pl.pallas_call(
        functools.partial(_odd_kernel, depth_idx=i, layer_idx=j),
        grid=(N_ROWS // rows,),
        in_specs=[_rows(rows, D_MODEL)] + [_layer_spec(a, j) for a in layer_params] + cast_in,
        out_specs=[_rows(rows, D_MODEL)] + cast_out,
        out_shape=[jax.ShapeDtypeStruct((N_ROWS, D_MODEL), f32)] + cast_shapes,
        scratch_shapes=[
            pltpu.VMEM((chunk_rows + POOL_HALO, D_POOL), f32),
            pltpu.VMEM((SGU_HEADS, chunk_rows, SGU_HEAD_DIM), f32),
            pltpu.VMEM((SGU_HEADS, chunk_rows, SGU_HEAD_DIM), f32),
        ],
        compiler_params=_params(),
        name="odd_mixer",
    )(x, *layer_params, ffn_w_up, ffn_w_down)


def _ffn_chunks():
    out, c0 = [], 0
    while c0 < D_FF:
        cw = min(256, D_FF - c0)
        out.append((c0, cw))
        c0 += cw
    return out


def _ffn_kernel(x_ref, g_ref, wup_ref, cw_ref, cb_ref, wdn_ref, gf_ref, o_ref, up_ref, act_ref,
                yt_ref=None, *, depth_idx):
    i = pl.program_id(0)
    rows = x_ref.shape[0]
    tq = rows // BATCH

    @pl.when(i == 0)
    def _():
        up_ref[...] = jnp.zeros_like(up_ref)

    x = x_ref[...]
    h = _rms(x, g_ref[depth_idx:depth_idx + 1, :]).astype(bf16)

    def conv(off, cw):
        cols = slice(off, off + cw)
        up = _dot(h, wup_ref[:, cols])
        ext = jnp.concatenate([up_ref[:, cols], up], axis=0)
        up_ref[:, cols] = up[rows - CONV_HALO:rows, :]
        y = ext[0:rows, :] * cw_ref[0:1, cols]
        for k in range(1, CONV_WIDTH):
            y = y + ext[k * BATCH:k * BATCH + rows, :] * cw_ref[k:k + 1, cols]
        return y + cb_ref[depth_idx:depth_idx + 1, cols]

    for c0, cw in _ffn_chunks():
        gate = conv(c0, cw)
        val = conv(D_FF + c0, cw)
        act_ref[:, c0:c0 + cw] = (jax.nn.silu(gate) * val).astype(bf16)

    out = x + _dot(act_ref[...], wdn_ref[...])
    if yt_ref is None:
        o_ref[...] = out
    else:
        out = _rms(out, gf_ref[...])
        for c in range(D_MODEL // LANES):
            yt_ref[c] = out[:, c * LANES:(c + 1) * LANES]
        for b in range(BATCH):
            for c in range(D_MODEL // LANES):
                o_ref[b, :, c * LANES:(c + 1) * LANES] = yt_ref[c, pl.ds(b, tq, stride=BATCH), :]


def _conv_ffn(x, j, layer_params, g_final, final):
    rows = FFN_TQ * BATCH
    scratch = [pltpu.VMEM((CONV_HALO, 2 * D_FF), f32), pltpu.VMEM((rows, D_FF), bf16)]
    if final:
        scratch.append(pltpu.VMEM((D_MODEL // LANES, rows, LANES), f32))
        out_spec = _batch_rows(FFN_TQ, D_MODEL)
        out_shape = jax.ShapeDtypeStruct((BATCH, SEQ, D_MODEL), f32)
    else:
        out_spec = _rows(rows, D_MODEL)
        out_shape = jax.ShapeDtypeStruct((N_ROWS, D_MODEL), f32)
    return pl.pallas_call(
        functools.partial(_ffn_kernel, depth_idx=j),
        grid=(N_ROWS // rows,),
        in_specs=([_rows(rows, D_MODEL)] + [_layer_spec(a, j) for a in layer_params]
                  + [_layer_spec(g_final, 0)]),
        out_specs=out_spec,
        out_shape=out_shape,
        scratch_shapes=scratch,
        compiler_params=_params(),
        name="conv_ffn",
    )(x, *layer_params, g_final)


def kernel(x, norm_mix_g, even_w_in, even_conv_w, ssm_log_step, ssm_a_re, ssm_a_im, ssm_b_re, ssm_b_im, ssm_c_re, ssm_c_im, ssm_d, ssm_glu_w, ssm_glu_b, even_w_out, odd_w_in, pool_w, pool_scale, sgu_norm_g, sgu_w, sgu_b, odd_w_out, norm_ffn_g, ffn_w_up, ffn_conv_w, ffn_conv_b, ffn_w_down, norm_final_g):
    assert x.shape == (BATCH, SEQ, D_MODEL)
    n_odd = pool_w.shape[0]
    emat, qmat, tmat, l_re, l_im, cast = _s5_matrices(
        ssm_log_step, ssm_a_re, ssm_a_im, ssm_b_re, ssm_b_im, ssm_c_re, ssm_c_im,
        (even_w_in, ssm_glu_w, even_w_out, odd_w_in,
         pool_w.reshape(n_odd, len(POOL_WINDOWS) * POOL_GROUP, POOL_GROUP), odd_w_out))
    even_w_in16, glu_w16, even_w_out16, odd_w_in16, pool_w16, odd_w_out16 = cast
    even_params = (
        norm_mix_g, even_w_in16, even_conv_w,
        l_re, l_im, emat, qmat, tmat, ssm_d,
        glu_w16, ssm_glu_b, even_w_out16)

    odd_params = (
        norm_mix_g, odd_w_in16, pool_w16.reshape(pool_w.shape),
        pool_scale, sgu_norm_g, sgu_w, jnp.swapaxes(sgu_b, 1, 2),
        odd_w_out16)

    g_final = norm_final_g.reshape(1, D_MODEL)

    xt = x
    for i in range(DEPTH):
        j = i // 2
        if i % 2 == 0:
            xt, w_up, w_down = _even_mixer(xt, i, j, even_params, ffn_w_up, ffn_w_down,
                                           batch_major_in=(i == 0))
        else:
            xt, w_up, w_down = _odd_mixer(xt, i, j, odd_params, ffn_w_up, ffn_w_down)
        ffn_params = (norm_ffn_g, w_up, ffn_conv_w, ffn_conv_b, w_down)
        xt = _conv_ffn(xt, i, ffn_params, g_final, final=(i == DEPTH - 1))
    return xt
```

```python
import functools

import jax
import jax.numpy as jnp
from jax import lax
from jax.experimental import pallas as pl
from jax.experimental.pallas import tpu as pltpu

D_MODEL = 1024
BATCH = 8
SEQ = 4096
DEPTH = 4
D_CONV = D_MODEL // 2
CONV_WIDTH = 3
D_SSM = D_MODEL // 2
SSM_GROUP = 16
N_SSM_GROUPS = D_SSM // SSM_GROUP
SSM_STATE = 64
D_POOL = D_MODEL // 2
POOL_WINDOWS = (2, 4, 8, 16)
POOL_GROUP = D_POOL // len(POOL_WINDOWS)
D_SGU = D_MODEL // 2
SGU_HEADS = 4
SGU_HEAD_DIM = D_SGU // SGU_HEADS
CHUNK = 128
D_FF = ((8 * D_MODEL // 3 + 127) // 128) * 128
EPS = 1e-6

N_ROWS = SEQ * BATCH
LANES = 128
V7X_VMEM_BYTES = 64 * 1024 * 1024
VMEM_LIMIT_BYTES = V7X_VMEM_BYTES - 4 * 1024 * 1024

CONV_HALO = (CONV_WIDTH - 1) * BATCH
POOL_HALO = max(POOL_WINDOWS) * BATCH
SSM_Q = 4
SLAB_GROUPS = LANES // SSM_GROUP
N_SLABS = N_SSM_GROUPS // SLAB_GROUPS
SLAB_STATES = SLAB_GROUPS * SSM_STATE
SSM_COLS = 2 * N_SLABS * SLAB_STATES

EVEN_TQ = 128
EVEN_TQ_BATCH_MAJOR = 128
EVEN_SUB_TQ = 128
FFN_TQ = 128
FFN_CAST_BLOCKS = 16
ODD_TQ = 2 * CHUNK

f32 = jnp.float32
bf16 = jnp.bfloat16


def _rms(x, g):
    return x * lax.rsqrt(jnp.mean(x * x, axis=-1, keepdims=True) + EPS) * g


def _dot(a, b):
    return jnp.dot(a, b, preferred_element_type=f32)


def _layer_spec(arr, j):
    nd = arr.ndim
    if nd == 2:
        return pl.BlockSpec(arr.shape, lambda i: (0, 0))
    return pl.BlockSpec((None,) + arr.shape[1:], lambda i: (j,) + (0,) * (nd - 1),
                        pipeline_mode=pl.Buffered(1))


def _rows(r, d):
    return pl.BlockSpec((r, d), lambda i: (i, 0))


def _batch_rows(tq, d):
    return pl.BlockSpec((BATCH, tq, d), lambda i: (0, i, 0))


def _ffn_weight_cast_specs(n_steps, layer):
    up_rows, dn_rows = D_MODEL // FFN_CAST_BLOCKS, D_FF // FFN_CAST_BLOCKS
    blk = lambda i: i * FFN_CAST_BLOCKS // n_steps
    in_specs = [pl.BlockSpec((None, up_rows, 2 * D_FF), lambda i: (layer, blk(i), 0)),
                pl.BlockSpec((None, dn_rows, D_MODEL), lambda i: (layer, blk(i), 0))]
    out_specs = [pl.BlockSpec((up_rows, 2 * D_FF), lambda i: (blk(i), 0)),
                 pl.BlockSpec((dn_rows, D_MODEL), lambda i: (blk(i), 0))]
    out_shapes = [jax.ShapeDtypeStruct((D_MODEL, 2 * D_FF), bf16),
                  jax.ShapeDtypeStruct((D_FF, D_MODEL), bf16)]
    return in_specs, out_specs, out_shapes


def _params():
    return pltpu.CompilerParams(dimension_semantics=("arbitrary",),
                                vmem_limit_bytes=VMEM_LIMIT_BYTES)


def _lam_power(m, a_re, a_im, step):
    mag = jnp.exp((m * a_re) * step)
    ang = (m * a_im) * step
    return mag * jnp.cos(ang), mag * jnp.sin(ang)


def _s5_matrix_kernel(lsr_ref, arr_ref, air_ref, btre_ref, btim_ref, cre_ref, cim_ref,
                      lsc_ref, arc_ref, aic_ref, *refs, n_cast):
    e_ref, q_ref, t_ref, lre_ref, lim_ref = refs[n_cast:n_cast + 5]
    for src, dst in zip(refs[:n_cast], refs[n_cast + 5:]):
        dst[...] = src[...].astype(bf16)

    rows, p = SLAB_GROUPS * SSM_GROUP, SSM_STATE
    step = jnp.exp(lsr_ref[0])
    a_re = arr_ref[0]
    a_im = air_ref[0]
    power = lambda m: _lam_power(m, a_re, a_im, step)

    l_re, l_im = power(1)
    n_re = l_re - 1.0
    den = a_re * a_re + a_im * a_im
    k_re = (n_re * a_re + l_im * a_im) / den
    k_im = (l_im * a_re - n_re * a_im) / den
    bt_re = btre_ref[0]
    bt_im = btim_ref[0]
    bb_re = k_re * bt_re - k_im * bt_im
    bb_im = k_re * bt_im + k_im * bt_re
    c_re = cre_ref[0]
    c_im = cim_ref[0]

    col = lax.broadcasted_iota(jnp.int32, (p, SLAB_STATES), 1)
    spread = (lax.bitwise_and(col, p - 1) == lax.broadcasted_iota(jnp.int32, (p, SLAB_STATES), 0)
              ).astype(bf16)
    same_group = (
        lax.shift_right_logical(lax.broadcasted_iota(jnp.int32, (rows, SLAB_STATES), 0),
                                SSM_GROUP.bit_length() - 1)
        == lax.shift_right_logical(lax.broadcasted_iota(jnp.int32, (rows, SLAB_STATES), 1),
                                   SSM_STATE.bit_length() - 1))

    def tile(v16):
        return jnp.where(same_group, _dot(v16, spread), 0.0)

    def split(v):
        hi = v.astype(bf16)
        return hi, (v - hi.astype(f32)).astype(bf16)

    def tile_split(v):
        hi, lo = split(v)
        return tile(hi), tile(lo)

    def dot_nt(a, b):
        return lax.dot_general(a.astype(bf16), b.astype(bf16), (((1,), (1,)), ((), ())),
                               preferred_element_type=f32)

    def dot_nt3(a, b):
        return dot_nt(a[0], b[0]) + dot_nt(a[0], b[1]) + dot_nt(a[1], b[0])

    bb_re_t, bb_im_t = tile_split(bb_re), tile_split(bb_im)
    t_blocks = []
    for m in range(SSM_Q + 1):
        p_re, p_im = power(m)
        cp_re_t = tile_split(c_re * p_re - c_im * p_im)
        cp_im_t = tile_split(c_re * p_im + c_im * p_re)
        if m < SSM_Q:
            i = SSM_Q - 1 - m
            e_ref[0, i * rows:(i + 1) * rows, 0:SLAB_STATES] = (
                tile((p_re * bb_re - p_im * bb_im).astype(bf16)).astype(bf16))
            e_ref[0, i * rows:(i + 1) * rows, SLAB_STATES:] = (
                tile((p_re * bb_im + p_im * bb_re).astype(bf16)).astype(bf16))
            t_blocks.append(dot_nt3(bb_re_t, cp_re_t) - dot_nt3(bb_im_t, cp_im_t))
        if m >= 1:
            j = m - 1
            q_ref[0, 0:SLAB_STATES, j * rows:(j + 1) * rows] = cp_re_t[0].T.astype(bf16)
            q_ref[0, SLAB_STATES:, j * rows:(j + 1) * rows] = (-cp_im_t[0]).T.astype(bf16)
    for i in range(SSM_Q):
        for j in range(SSM_Q):
            blk = t_blocks[j - i] if j >= i else jnp.zeros((rows, rows), f32)
            t_ref[0, i * rows:(i + 1) * rows, j * rows:(j + 1) * rows] = blk.astype(bf16)

    lq_re, lq_im = _lam_power(SSM_Q, arc_ref[0], aic_ref[0], jnp.exp(lsc_ref[0]))
    lre_ref[0] = lq_re
    lim_ref[0] = lq_im


def _s5_matrices(log_step, a_re, a_im, b_re, b_im, c_re, c_im, mixer_weights):
    nl = log_step.shape[0]
    n = nl * N_SLABS
    rows, p, hg = SLAB_GROUPS * SSM_GROUP, SSM_STATE, SSM_GROUP
    per_row = lambda v: jnp.repeat(v, hg, axis=1).reshape(n, rows, -1)
    by_row = lambda v: v.reshape(n, rows, p)
    compact = lambda v: v.reshape(n, SLAB_GROUPS, -1)

    def spec(*dims):
        return pl.BlockSpec((1,) + dims, lambda i: (i,) + (0,) * len(dims))

    flat = [w.reshape(-1, w.shape[-1]) for w in mixer_weights]
    cast_specs = [pl.BlockSpec((w.shape[0] // n, w.shape[1]), lambda i: (i, 0)) for w in flat]

    wide = 2 * SLAB_STATES
    emat, qmat, tmat, l_re, l_im, *cast = pl.pallas_call(
        functools.partial(_s5_matrix_kernel, n_cast=len(flat)),
        grid=(n,),
        in_specs=[spec(rows, 1), spec(rows, p), spec(rows, p), spec(rows, p), spec(rows, p),
                  spec(rows, p), spec(rows, p), spec(SLAB_GROUPS, 1), spec(SLAB_GROUPS, p),
                  spec(SLAB_GROUPS, p)] + cast_specs,
        out_specs=[spec(SSM_Q * rows, wide), spec(wide, SSM_Q * rows), spec(SSM_Q * rows, SSM_Q * rows),
                   spec(SLAB_GROUPS, p), spec(SLAB_GROUPS, p)] + cast_specs,
        out_shape=[jax.ShapeDtypeStruct((n, SSM_Q * rows, wide), bf16),
                   jax.ShapeDtypeStruct((n, wide, SSM_Q * rows), bf16),
                   jax.ShapeDtypeStruct((n, SSM_Q * rows, SSM_Q * rows), bf16),
                   jax.ShapeDtypeStruct((n, SLAB_GROUPS, p), f32),
                   jax.ShapeDtypeStruct((n, SLAB_GROUPS, p), f32)]
                  + [jax.ShapeDtypeStruct(w.shape, bf16) for w in flat],
        compiler_params=_params(),
        name="s5_matrices",
    )(per_row(log_step[:, :, None]), per_row(a_re), per_row(a_im),
      by_row(jnp.swapaxes(b_re, 2, 3)), by_row(jnp.swapaxes(b_im, 2, 3)), by_row(c_re), by_row(c_im),
      compact(log_step), compact(a_re), compact(a_im), *flat)
    split = lambda m: m.reshape((nl, N_SLABS) + m.shape[1:])
    lam = lambda v: v.reshape(nl, 1, N_SSM_GROUPS * SSM_STATE)
    cast = tuple(c.reshape(w.shape) for c, w in zip(cast, mixer_weights))
    return split(emat), split(qmat), split(tmat), lam(l_re), lam(l_im), cast


def _even_kernel(x_ref, g_ref, win_ref, cw_ref, lr_ref, li_ref, em_ref, qm_ref, tm_ref,
                 dsk_ref, gw_ref, gb_ref, wout_ref, fup_ref, fdn_ref,
                 o_ref, fup16_ref, fdn16_ref, cx_ref, bu_ref, st_ref,
                 xt_ref=None, *, depth_idx, layer_idx):
    i = pl.program_id(0)
    fup16_ref[...] = fup_ref[...].astype(bf16)
    fdn16_ref[...] = fdn_ref[...].astype(bf16)
    gain = g_ref[depth_idx:depth_idx + 1, :]
    d_skip = dsk_ref[layer_idx:layer_idx + 1, :]
    glu_b = gb_ref[layer_idx:layer_idx + 1, :]
    tile_rows = o_ref.shape[0]

    @pl.when(i == 0)
    def _():
        cx_ref[0:CONV_HALO, :] = jnp.zeros((CONV_HALO, D_CONV), f32)
        st_ref[...] = jnp.zeros_like(st_ref)

    if xt_ref is not None:
        for b in range(BATCH):
            for c in range(D_MODEL // LANES):
                xt_ref[c, pl.ds(b, tile_rows // BATCH, stride=BATCH), :] = (
                    x_ref[b, :, c * LANES:(c + 1) * LANES])

    rows = EVEN_SUB_TQ * BATCH
    tq = EVEN_SUB_TQ
    n_sub = tile_rows // rows
    assert D_CONV == D_SSM and N_SLABS == 4

    def normed(sub):
        r_lo = sub * rows
        if xt_ref is None:
            x = x_ref[r_lo:r_lo + rows, :]
        else:
            x = jnp.concatenate(
                [xt_ref[c, r_lo:r_lo + rows, :] for c in range(D_MODEL // LANES)], axis=1)
        return x, _rms(x, gain).astype(bf16)

    def in_slice(h, q):
        return _dot(h, win_ref[:, q * D_CONV:(q + 1) * D_CONV])

    x, h = normed(0)
    xa, ba, ca, u = [in_slice(h, q) for q in range(4)]
    ya = _even_conv(xa, ba, ca, cw_ref, cx_ref)
    u4 = _fold_time(u)
    for s in range(N_SLABS):
        _s5_expand(u4[s], s, em_ref, bu_ref)

    for sub in range(n_sub):
        last = sub + 1 == n_sub
        if not last:
            x_n, h_n = normed(sub + 1)
            parts_n = []
        y4 = []
        for s in range(N_SLABS):
            _s5_scan(s, tq // SSM_Q, lr_ref, li_ref, bu_ref, st_ref)
            if not last:
                parts_n.append(in_slice(h_n, s))
            y4.append(_s5_readout(s, u4[s], qm_ref, tm_ref, bu_ref))
        y = jax.nn.gelu(_unfold_time(y4) + d_skip * u)
        if not last:
            ya_n = _even_conv(parts_n[0], parts_n[1], parts_n[2], cw_ref, cx_ref)
            u_n = parts_n[3]
            u4_n = _fold_time(u_n)
            for s in range(N_SLABS // 2):
                _s5_expand(u4_n[s], s, em_ref, bu_ref)
        gate = _dot(y.astype(bf16), gw_ref[...]) + glu_b
        if not last:
            for s in range(N_SLABS // 2, N_SLABS):
                _s5_expand(u4_n[s], s, em_ref, bu_ref)
        yb = y * jax.nn.sigmoid(gate)
        mix = _dot(ya.astype(bf16), wout_ref[0:D_CONV, :]) + _dot(yb.astype(bf16), wout_ref[D_CONV:, :])
        o_ref[sub * rows:(sub + 1) * rows, :] = x + mix
        if not last:
            x, ya, u, u4 = x_n, ya_n, u_n, u4_n


def _even_conv(xa, ba, ca, cw_ref, cx_ref):
    rows = xa.shape[0]
    cx_ref[CONV_HALO:CONV_HALO + rows, :] = ca * xa
    conv = cx_ref[0:rows, :] * cw_ref[0:1, :]
    for k in range(1, CONV_WIDTH):
        conv = conv + cx_ref[k * BATCH:k * BATCH + rows, :] * cw_ref[k:k + 1, :]
    ya = ba * conv
    cx_ref[0:CONV_HALO, :] = cx_ref[rows:rows + CONV_HALO, :]
    return ya


def _fold_time(u):
    n_chunks = u.shape[0] // (SSM_Q * BATCH)
    at_offset = [
        jnp.concatenate([u[(SSM_Q * k + i) * BATCH:(SSM_Q * k + i + 1) * BATCH, :]
                         for k in range(n_chunks)], axis=0)
        for i in range(SSM_Q)]
    return [jnp.concatenate([a[:, s * LANES:(s + 1) * LANES] for a in at_offset], axis=1).astype(bf16)
            for s in range(N_SLABS)]


def _unfold_time(y4):
    n_chunks = y4[0].shape[0] // BATCH
    at_offset = [jnp.concatenate([y[:, j * LANES:(j + 1) * LANES] for y in y4], axis=1)
                 for j in range(SSM_Q)]
    return jnp.concatenate([at_offset[j][k * BATCH:(k + 1) * BATCH, :]
                            for k in range(n_chunks) for j in range(SSM_Q)], axis=0)


def _s5_expand(u4s, s, em_ref, bu_ref):
    c0 = 2 * SLAB_STATES * s
    bu_ref[:, c0:c0 + 2 * SLAB_STATES] = _dot(u4s, em_ref[s])


def _s5_scan(s, n_chunks, lr_ref, li_ref, bu_ref, st_ref):
    c_re = 2 * SLAB_STATES * s
    c_im = c_re + SLAB_STATES
    re = slice(c_re, c_re + SLAB_STATES)
    im = slice(c_im, c_im + SLAB_STATES)
    lam = slice(SLAB_STATES * s, SLAB_STATES * (s + 1))
    ar = jnp.broadcast_to(lr_ref[:, lam], (BATCH, SLAB_STATES))
    ai = jnp.broadcast_to(li_ref[:, lam], (BATCH, SLAB_STATES))
    s_re = st_ref[:, re]
    s_im = st_ref[:, im]
    for k in range(n_chunks):
        r = slice(k * BATCH, (k + 1) * BATCH)
        b_re = bu_ref[r, re]
        b_im = bu_ref[r, im]
        bu_ref[r, re] = s_re
        bu_ref[r, im] = s_im
        s_re, s_im = ar * s_re - ai * s_im + b_re, ar * s_im + ai * s_re + b_im
    st_ref[:, re] = s_re
    st_ref[:, im] = s_im


def _s5_readout(s, u4s, qm_ref, tm_ref, bu_ref):
    c_re = 2 * SLAB_STATES * s
    c_im = c_re + SLAB_STATES
    return (_dot(bu_ref[:, c_re:c_re + SLAB_STATES].astype(bf16), qm_ref[s, 0:SLAB_STATES, :])
            + _dot(bu_ref[:, c_im:c_im + SLAB_STATES].astype(bf16), qm_ref[s, SLAB_STATES:, :])
            + _dot(u4s, tm_ref[s]))


def _even_mixer(x, i, j, layer_params, ffn_w_up, ffn_w_down, batch_major_in):
    tq = EVEN_TQ_BATCH_MAJOR if batch_major_in else EVEN_TQ
    rows = tq * BATCH
    cast_in, cast_out, cast_shapes = _ffn_weight_cast_specs(N_ROWS // rows, i)
    sub_rows = EVEN_SUB_TQ * BATCH
    x_spec = _batch_rows(tq, D_MODEL) if batch_major_in else _rows(rows, D_MODEL)
    scratch = [
        pltpu.VMEM((sub_rows + CONV_HALO, D_CONV), f32),
        pltpu.VMEM((sub_rows // SSM_Q, SSM_COLS), f32),
        pltpu.VMEM((BATCH, SSM_COLS), f32),
    ]
    if batch_major_in:
        scratch.append(pltpu.VMEM((D_MODEL // LANES, rows, LANES), f32))
    return pl.pallas_call(
        functools.partial(_even_kernel, depth_idx=i, layer_idx=j),
        grid=(N_ROWS // rows,),
        in_specs=[x_spec] + [_layer_spec(a, j) for a in layer_params] + cast_in,
        out_specs=[_rows(rows, D_MODEL)] + cast_out,
        out_shape=[jax.ShapeDtypeStruct((N_ROWS, D_MODEL), f32)] + cast_shapes,
        scratch_shapes=scratch,
        compiler_params=_params(),
        name="even_mixer",
    )(x, *layer_params, ffn_w_up, ffn_w_down)


def _odd_kernel(x_ref, g_ref, win_ref, pw_ref, ps_ref, ng_ref, sw_ref, sb_ref, wout_ref,
                fup_ref, fdn_ref, o_ref, fup16_ref, fdn16_ref, z_ref, v_ref, m_ref,
                *, depth_idx, layer_idx):
    i = pl.program_id(0)
    fup16_ref[...] = fup_ref[...].astype(bf16)
    fdn16_ref[...] = fdn_ref[...].astype(bf16)
    tq = CHUNK
    rows = tq * BATCH
    n_chunks = x_ref.shape[0] // rows

    @pl.when(i == 0)
    def _():
        z_ref[0:POOL_HALO, :] = jnp.zeros((POOL_HALO, D_POOL), f32)

    def front(c):
        x = x_ref[c * rows:(c + 1) * rows, :]
        h = _rms(x, g_ref[depth_idx:depth_idx + 1, :]).astype(bf16)
        proj = _dot(h, win_ref[...])
        uv = jax.nn.gelu(proj[:, D_POOL:])
        return x, proj[:, 0:D_POOL], uv[:, 0:D_SGU], uv[:, D_SGU:]

    cur = front(0)
    for c in range(n_chunks):
        nxt = front(c + 1) if c + 1 < n_chunks else None
        x, z, su, sv = cur
        mix = _odd_mix(i * n_chunks + c, z, su, sv, pw_ref, ps_ref, ng_ref, sw_ref, sb_ref, wout_ref,
                       z_ref, v_ref, m_ref, layer_idx)
        o_ref[c * rows:(c + 1) * rows, :] = x + mix
        cur = nxt


def _odd_mix(chunk_idx, z, su, sv, pw_ref, ps_ref, ng_ref, sw_ref, sb_ref, wout_ref,
             z_ref, v_ref, m_ref, layer_idx):
    rows = z.shape[0]
    tq = rows // BATCH

    z_ref[POOL_HALO:POOL_HALO + rows, :] = z
    count = (chunk_idx * tq + 1 + lax.broadcasted_iota(jnp.int32, (rows, 1), 0) // BATCH).astype(f32)
    yc = []
    for gi, win in enumerate(POOL_WINDOWS):
        cols = slice(gi * POOL_GROUP, (gi + 1) * POOL_GROUP)
        e = z_ref[:, cols]
        span = 1
        while span < win:
            sh = span * BATCH
            e = e[sh:, :] + e[:-sh, :]
            span *= 2
        wsum = e[e.shape[0] - rows:, :]
        pooled = wsum / jnp.minimum(count, float(win)) - z[:, cols]
        yc.append(_dot(pooled.astype(bf16), pw_ref[gi]))
    yc = jnp.concatenate(yc, axis=1) * ps_ref[layer_idx:layer_idx + 1, :]
    z_ref[0:POOL_HALO, :] = z_ref[rows:rows + POOL_HALO, :]

    v = _rms(sv, ng_ref[layer_idx:layer_idx + 1, :])
    tri = lax.broadcasted_iota(jnp.int32, (CHUNK, CHUNK), 0) >= lax.broadcasted_iota(
        jnp.int32, (CHUNK, CHUNK), 1)
    for hd in range(SGU_HEADS):
        v_ref[hd] = v[:, hd * SGU_HEAD_DIM:(hd + 1) * SGU_HEAD_DIM]
    for hd in range(SGU_HEADS):
        w_s = jnp.where(tri, sw_ref[hd], 0.0).astype(bf16)
        v_b = jnp.concatenate(
            [v_ref[hd, pl.ds(b, CHUNK, stride=BATCH), :].astype(bf16) for b in range(BATCH)],
            axis=1)
        mixed_b = _dot(w_s, v_b) + sb_ref[:, hd:hd + 1]
        for b in range(BATCH):
            m_ref[hd, pl.ds(b, CHUNK, stride=BATCH), :] = mixed_b[:, b * SGU_HEAD_DIM:(b + 1) * SGU_HEAD_DIM]
    yd = su * jnp.concatenate([m_ref[hd] for hd in range(SGU_HEADS)], axis=1)

    return _dot(yc.astype(bf16), wout_ref[0:D_POOL, :]) + _dot(yd.astype(bf16), wout_ref[D_POOL:, :])


def _odd_mixer(x, i, j, layer_params, ffn_w_up, ffn_w_down):
    rows = ODD_TQ * BATCH
    chunk_rows = CHUNK * BATCH
    cast_in, cast_out, cast_shapes = _ffn_weight_cast_specs(N_ROWS // rows, i)
    return pl.pallas_call(
        functools.partial(_odd_kernel, depth_idx=i, layer_idx=j),
        grid=(N_ROWS // rows,),
        in_specs=[_rows(rows, D_MODEL)] + [_layer_spec(a, j) for a in layer_params] + cast_in,
        out_specs=[_rows(rows, D_MODEL)] + cast_out,
        out_shape=[jax.ShapeDtypeStruct((N_ROWS, D_MODEL), f32)] + cast_shapes,
        scratch_shapes=[
            pltpu.VMEM((chunk_rows + POOL_HALO, D_POOL), f32),
            pltpu.VMEM((SGU_HEADS, chunk_rows, SGU_HEAD_DIM), f32),
            pltpu.VMEM((SGU_HEADS, chunk_rows, SGU_HEAD_DIM), f32),
        ],
        compiler_params=_params(),
        name="odd_mixer",
    )(x, *layer_params, ffn_w_up, ffn_w_down)


def _ffn_chunks():
    out, c0 = [], 0
    while c0 < D_FF:
        cw = min(256, D_FF - c0)
        out.append((c0, cw))
        c0 += cw
    return out


def _ffn_kernel(x_ref, g_ref, wup_ref, cw_ref, cb_ref, wdn_ref, gf_ref, o_ref, up_ref, act_ref,
                yt_ref=None, *, depth_idx):
    i = pl.program_id(0)
    rows = x_ref.shape[0]
    tq = rows // BATCH

    @pl.when(i == 0)
    def _():
        up_ref[...] = jnp.zeros_like(up_ref)

    x = x_ref[...]
    h = _rms(x, g_ref[depth_idx:depth_idx + 1, :]).astype(bf16)

    def conv(off, cw):
        cols = slice(off, off + cw)
        up = _dot(h, wup_ref[:, cols])
        ext = jnp.concatenate([up_ref[:, cols], up], axis=0)
        up_ref[:, cols] = up[rows - CONV_HALO:rows, :]
        y = ext[0:rows, :] * cw_ref[0:1, cols]
        for k in range(1, CONV_WIDTH):
            y = y + ext[k * BATCH:k * BATCH + rows, :] * cw_ref[k:k + 1, cols]
        return y + cb_ref[depth_idx:depth_idx + 1, cols]

    for c0, cw in _ffn_chunks():
        gate = conv(c0, cw)
        val = conv(D_FF + c0, cw)
        act_ref[:, c0:c0 + cw] = (jax.nn.silu(gate) * val).astype(bf16)

    out = x + _dot(act_ref[...], wdn_ref[...])
    if yt_ref is None:
        o_ref[...] = out
    else:
        out = _rms(out, gf_ref[...])
        for c in range(D_MODEL // LANES):
            yt_ref[c] = out[:, c * LANES:(c + 1) * LANES]
        for b in range(BATCH):
            for c in range(D_MODEL // LANES):
                o_ref[b, :, c * LANES:(c + 1) * LANES] = yt_ref[c, pl.ds(b, tq, stride=BATCH), :]


def _conv_ffn(x, j, layer_params, g_final, final):
    rows = FFN_TQ * BATCH
    scratch = [pltpu.VMEM((CONV_HALO, 2 * D_FF), f32), pltpu.VMEM((rows, D_FF), bf16)]
    if final:
        scratch.append(pltpu.VMEM((D_MODEL // LANES, rows, LANES), f32))
        out_spec = _batch_rows(FFN_TQ, D_MODEL)
        out_shape = jax.ShapeDtypeStruct((BATCH, SEQ, D_MODEL), f32)
    else:
        out_spec = _rows(rows, D_MODEL)
        out_shape = jax.ShapeDtypeStruct((N_ROWS, D_MODEL), f32)
    return pl.pallas_call(
        functools.partial(_ffn_kernel, depth_idx=j),
        grid=(N_ROWS // rows,),
        in_specs=([_rows(rows, D_MODEL)] + [_layer_spec(a, j) for a in layer_params]
                  + [_layer_spec(g_final, 0)]),
        out_specs=out_spec,
        out_shape=out_shape,
        scratch_shapes=scratch,
        compiler_params=_params(),
        name="conv_ffn",
    )(x, *layer_params, g_final)


def kernel(x, norm_mix_g, even_w_in, even_conv_w, ssm_log_step, ssm_a_re, ssm_a_im, ssm_b_re, ssm_b_im, ssm_c_re, ssm_c_im, ssm_d, ssm_glu_w, ssm_glu_b, even_w_out, odd_w_in, pool_w, pool_scale, sgu_norm_g, sgu_w, sgu_b, odd_w_out, norm_ffn_g, ffn_w_up, ffn_conv_w, ffn_conv_b, ffn_w_down, norm_final_g):
    assert x.shape == (BATCH, SEQ, D_MODEL)
    n_odd = pool_w.shape[0]
    emat, qmat, tmat, l_re, l_im, cast = _s5_matrices(
        ssm_log_step, ssm_a_re, ssm_a_im, ssm_b_re, ssm_b_im, ssm_c_re, ssm_c_im,
        (even_w_in, ssm_glu_w, even_w_out, odd_w_in,
         pool_w.reshape(n_odd, len(POOL_WINDOWS) * POOL_GROUP, POOL_GROUP), odd_w_out))
    even_w_in16, glu_w16, even_w_out16, odd_w_in16, pool_w16, odd_w_out16 = cast
    even_params = (
        norm_mix_g, even_w_in16, even_conv_w,
        l_re, l_im, emat, qmat, tmat, ssm_d,
        glu_w16, ssm_glu_b, even_w_out16)

    odd_params = (
        norm_mix_g, odd_w_in16, pool_w16.reshape(pool_w.shape),
        pool_scale, sgu_norm_g, sgu_w, jnp.swapaxes(sgu_b, 1, 2),
        odd_w_out16)

    g_final = norm_final_g.reshape(1, D_MODEL)

    xt = x
    for i in range(DEPTH):
        j = i // 2
        if i % 2 == 0:
            xt, w_up, w_down = _even_mixer(xt, i, j, even_params, ffn_w_up, ffn_w_down,
                                           batch_major_in=(i == 0))
        else:
            xt, w_up, w_down = _odd_mixer(xt, i, j, odd_params, ffn_w_up, ffn_w_down)
        ffn_params = (norm_ffn_g, w_up, ffn_conv_w, ffn_conv_b, w_down)
        xt = _conv_ffn(xt, i, ffn_params, g_final, final=(i == DEPTH - 1))
    return xt
```
